```python
import math
import jax, jax.numpy as jnp
from jax import lax
import numpy as np

D_MODEL = 4096
BATCH = 4
SEQ = 4096
DEPTH = 4

N_MIXERS = 3
N_A = len(range(0, DEPTH, N_MIXERS))
N_B = len(range(1, DEPTH, N_MIXERS))
N_C = len(range(2, DEPTH, N_MIXERS))

A_CHUNK = 128
A_GROUPS = 16
A_WIDTH = D_MODEL
A_GROUP_DIM = A_WIDTH // A_GROUPS

B_PATTERNS = ((128, 1), (512, 4), (2048, 16))
B_HEADS_PER_GROUP = 16
B_HEAD_DIM = 128
B_ROT_DIM = B_HEAD_DIM // 4
ROPE_THETA = 500000.0
B_QBLOCK = 128

C_WINDOWS = (2, 4, 8, 16)
C_WIDTH = D_MODEL
C_GROUP_DIM = C_WIDTH // len(C_WINDOWS)

D_FF = 11008
CONV_WIDTH = 3

LN_EPS = 1e-5
DEEPNORM_ALPHA = (2 * DEPTH) ** 0.25
DEEPNORM_BETA = (8 * DEPTH) ** -0.25

kernel_name = "hybrid_gmlp_dilated_attn_pool_trunk"


def layer_norm(x, g, b):
    xf = x.astype(jnp.float32)
    mu = jnp.mean(xf, axis=-1, keepdims=True)
    var = jnp.mean(jnp.square(xf - mu), axis=-1, keepdims=True)
    y = (xf - mu) * lax.rsqrt(var + LN_EPS)
    return (y * g.astype(jnp.float32) + b.astype(jnp.float32)).astype(x.dtype)


def partial_rotary(t, positions):
    half = B_ROT_DIM // 2
    inv_freq = jnp.float32(ROPE_THETA) ** (-jnp.arange(half, dtype=jnp.float32) * 2.0 / B_ROT_DIM)
    ang = positions.astype(jnp.float32)[..., None] * inv_freq
    cos = jnp.cos(ang)[:, :, None, :]
    sin = jnp.sin(ang)[:, :, None, :]
    tf = t.astype(jnp.float32)
    t1 = tf[..., :half]
    t2 = tf[..., half:B_ROT_DIM]
    out = jnp.concatenate([t1 * cos - t2 * sin, t2 * cos + t1 * sin, tf[..., B_ROT_DIM:]], axis=-1)
    return out.astype(t.dtype)


def dilated_window_attention(q, k, v, window, dilation):
    B, S, H, D = q.shape
    L = S // dilation
    reach = window // dilation
    nb = -(-L // B_QBLOCK)
    Lp = nb * B_QBLOCK
    Z = B * dilation

    def to_blocks(t):
        t = t.reshape(B, L, dilation, H, D).transpose(0, 2, 1, 3, 4).reshape(Z, L, H, D)
        t = jnp.pad(t, ((0, 0), (0, Lp - L), (0, 0), (0, 0)))
        return t.reshape(Z, nb, B_QBLOCK, H, D)

    def with_prev(t):
        prev = jnp.pad(t, ((0, 0), (1, 0), (0, 0), (0, 0), (0, 0)))[:, :nb]
        return jnp.concatenate([prev, t], axis=2)

    def from_blocks(t):
        trailing = t.shape[3:]
        t = t.reshape((Z, Lp) + trailing)[:, :L]
        t = t.reshape((B, dilation, L) + trailing)
        t = jnp.swapaxes(t, 1, 2)
        return t.reshape((B, S) + trailing)

    qb = to_blocks(q)
    kk = with_prev(to_blocks(k))
    vv = with_prev(to_blocks(v))
    s = jnp.einsum('znqhd,znkhd->znhqk', qb, kk, preferred_element_type=jnp.float32) * (D ** -0.5)
    qi = jnp.arange(B_QBLOCK)[:, None]
    kj = jnp.arange(2 * B_QBLOCK)[None, :]
    dist = qi + B_QBLOCK - kj
    band = (dist >= 0) & (dist <= reach)
    key_pos = jnp.arange(nb)[:, None, None] * B_QBLOCK + kj[None] - B_QBLOCK
    mask = band[None] & (key_pos >= 0)
    s = jnp.where(mask[None, :, None], s, -jnp.inf)
    m = jnp.max(s, axis=-1, keepdims=True)
    p = jnp.exp(s - m)
    den = jnp.sum(p, axis=-1, keepdims=True)
    o = jnp.einsum('znhqk,znkhd->znqhd', p, vv.astype(jnp.float32))
    o = o / den.transpose(0, 1, 3, 2, 4)
    lse = (m + jnp.log(den))[..., 0].transpose(0, 1, 3, 2)
    return from_blocks(o), from_blocks(lse)


def dilated_attention_mixer(x, positions, w_in, w_out):
    B, S, _ = x.shape
    G = len(B_PATTERNS)
    qkv = (x @ w_in).reshape(B, S, G, 3, B_HEADS_PER_GROUP, B_HEAD_DIM)
    outs, lses = [], []
    for g, (window, dilation) in enumerate(B_PATTERNS):
        q = partial_rotary(qkv[:, :, g, 0], positions)
        k = partial_rotary(qkv[:, :, g, 1], positions)
        o, lse = dilated_window_attention(q, k, qkv[:, :, g, 2], window, dilation)
        outs.append(o)
        lses.append(lse)
    wts = jax.nn.softmax(jnp.stack(lses), axis=0)
    o = jnp.sum(wts[..., None] * jnp.stack(outs), axis=0)
    return o.reshape(B, S, B_HEADS_PER_GROUP * B_HEAD_DIM).astype(x.dtype) @ w_out


def chunked_gating_mixer(x, w_in, ln_g, ln_b, w_s, b_s, w_out):
    B, S, _ = x.shape
    h = jax.nn.gelu(x @ w_in, approximate=False)
    u, v = jnp.split(h, 2, axis=-1)
    v = layer_norm(v, ln_g, ln_b)
    nc = S // A_CHUNK
    v = v.reshape(B, nc, A_CHUNK, A_GROUPS, A_GROUP_DIM)
    causal = jnp.tril(jnp.ones((A_CHUNK, A_CHUNK), dtype=bool))
    w = jnp.where(causal, w_s, 0.0).astype(v.dtype)
    mixed = jnp.einsum('gts,bnsgc->bntgc', w, v) + b_s.T[None, None, :, :, None].astype(v.dtype)
    y = u * mixed.reshape(B, S, A_WIDTH)
    return y @ w_out


def multiscale_pool_mixer(x, w_in, w_group, scale, w_out):
    B, S, _ = x.shape
    h = (x @ w_in).astype(jnp.float32).reshape(B, S, len(C_WINDOWS), C_GROUP_DIM)
    steps = jnp.arange(1, S + 1, dtype=jnp.float32)
    pooled = []
    for g, window in enumerate(C_WINDOWS):
        hg = h[:, :, g]
        cs = jnp.cumsum(hg, axis=1)
        lagged = jnp.pad(cs, ((0, 0), (window, 0), (0, 0)))[:, :S]
        mean = (cs - lagged) / jnp.minimum(steps, float(window))[None, :, None]
        pooled.append(mean - hg)
    p = jnp.stack(pooled, axis=2).astype(x.dtype)
    y = jnp.einsum('bsgc,gcd->bsgd', p, w_group) * scale.reshape(len(C_WINDOWS), C_GROUP_DIM)
    return y.reshape(B, S, C_WIDTH) @ w_out


def causal_depthwise_conv(h, w, b):
    S = h.shape[1]
    hp = jnp.pad(h, ((0, 0), (CONV_WIDTH - 1, 0), (0, 0)))
    return sum(w[j] * hp[:, j:j + S] for j in range(CONV_WIDTH)) + b


def conv_gated_ffn(x, w_up, conv_w, conv_b, w_down):
    h = causal_depthwise_conv(x @ w_up, conv_w, conv_b)
    gate, up = jnp.split(h, 2, axis=-1)
    return (jax.nn.silu(gate) * up) @ w_down


def setup_inputs(seed: int = 0) -> dict:
    key = jax.random.key(seed)
    ks = iter(jax.random.split(key, 32))

    def nrm(shape, scale):
        return jax.random.normal(next(ks), shape, jnp.float32) * scale

    x = nrm((BATCH, SEQ, D_MODEL), 1.0)
    offset = jax.random.randint(next(ks), (BATCH, 1), 0, 1024, dtype=jnp.int32)
    positions = (jnp.arange(SEQ, dtype=jnp.int32)[None, :] + offset).astype(jnp.int32)
    b_qkv = len(B_PATTERNS) * 3 * B_HEADS_PER_GROUP * B_HEAD_DIM
    b_out = B_HEADS_PER_GROUP * B_HEAD_DIM
    return {
        "x": x,
        "positions": positions,
        "a_w_in": nrm((N_A, D_MODEL, 2 * A_WIDTH), D_MODEL ** -0.5),
        "a_ln_g": 1.0 + nrm((N_A, A_WIDTH), 0.02),
        "a_ln_b": nrm((N_A, A_WIDTH), 0.02),
        "a_w_s": nrm((N_A, A_GROUPS, A_CHUNK, A_CHUNK), A_CHUNK ** -0.5),
        "a_b_s": 1.0 + nrm((N_A, A_GROUPS, A_CHUNK), 0.02),
        "a_w_out": nrm((N_A, A_WIDTH, D_MODEL), A_WIDTH ** -0.5 * DEEPNORM_BETA),
        "b_w_in": nrm((N_B, D_MODEL, b_qkv), D_MODEL ** -0.5),
        "b_w_out": nrm((N_B, b_out, D_MODEL), b_out ** -0.5 * DEEPNORM_BETA),
        "c_w_in": nrm((N_C, D_MODEL, C_WIDTH), D_MODEL ** -0.5),
        "c_w_group": nrm((N_C, len(C_WINDOWS), C_GROUP_DIM, C_GROUP_DIM), C_GROUP_DIM ** -0.5),
        "c_scale": 1.0 + nrm((N_C, C_WIDTH), 0.02),
        "c_w_out": nrm((N_C, C_WIDTH, D_MODEL), C_WIDTH ** -0.5 * DEEPNORM_BETA),
        "f_w_up": nrm((DEPTH, D_MODEL, 2 * D_FF), D_MODEL ** -0.5),
        "f_conv_w": nrm((DEPTH, CONV_WIDTH, 2 * D_FF), CONV_WIDTH ** -0.5),
        "f_conv_b": nrm((DEPTH, 2 * D_FF), 0.02),
        "f_w_down": nrm((DEPTH, D_FF, D_MODEL), D_FF ** -0.5 * DEEPNORM_BETA),
        "ln_mix_g": 1.0 + nrm((DEPTH, D_MODEL), 0.02),
        "ln_mix_b": nrm((DEPTH, D_MODEL), 0.02),
        "ln_ffn_g": 1.0 + nrm((DEPTH, D_MODEL), 0.02),
        "ln_ffn_b": nrm((DEPTH, D_MODEL), 0.02),
    }


def reference(x, positions, a_w_in, a_ln_g, a_ln_b, a_w_s, a_b_s, a_w_out,
              b_w_in, b_w_out, c_w_in, c_w_group, c_scale, c_w_out,
              f_w_up, f_conv_w, f_conv_b, f_w_down,
              ln_mix_g, ln_mix_b, ln_ffn_g, ln_ffn_b):
    for i in range(DEPTH):
        kind, j = i % N_MIXERS, i // N_MIXERS
        if kind == 0:
            y = chunked_gating_mixer(x, a_w_in[j], a_ln_g[j], a_ln_b[j], a_w_s[j], a_b_s[j], a_w_out[j])
        elif kind == 1:
            y = dilated_attention_mixer(x, positions, b_w_in[j], b_w_out[j])
        else:
            y = multiscale_pool_mixer(x, c_w_in[j], c_w_group[j], c_scale[j], c_w_out[j])
        x = layer_norm(DEEPNORM_ALPHA * x + y, ln_mix_g[i], ln_mix_b[i])
        f = conv_gated_ffn(x, f_w_up[i], f_conv_w[i], f_conv_b[i], f_w_down[i])
        x = layer_norm(DEEPNORM_ALPHA * x + f, ln_ffn_g[i], ln_ffn_b[i])
    return x
```

```python
import functools
import math

import jax
import jax.numpy as jnp
from jax import lax
from jax.experimental import pallas as pl
from jax.experimental.pallas import tpu as pltpu

F32 = jnp.float32
BF16 = jnp.bfloat16

D_MODEL = 4096
DEPTH = 4
N_MIXERS = 3

A_CHUNK = 128
A_GROUPS = 16
A_GROUP_DIM = D_MODEL // A_GROUPS

B_PATTERNS = ((128, 1), (512, 4), (2048, 16))
B_HEADS = 16
B_HEAD_DIM = 128
B_ROT_DIM = B_HEAD_DIM // 4
B_QBLOCK = 128
B_WIDTH = B_HEADS * B_HEAD_DIM
ROPE_THETA = 500000.0

C_WINDOWS = (2, 4, 8, 16)
C_GROUP_DIM = D_MODEL // len(C_WINDOWS)
C_HALO = 16

D_FF = 11008
FF_BLOCK = 512
D_FF_PAD = -(-D_FF // (2 * FF_BLOCK)) * (2 * FF_BLOCK)
CONV_WIDTH = 3
CARRY_ROWS = 8

LN_EPS = 1e-5
LN_ROWS = 32
DEEPNORM_ALPHA = (2 * DEPTH) ** 0.25

V7X_VMEM_LIMIT_BYTES = 56 * 1024 * 1024


def _params(*semantics):
    return pltpu.CompilerParams(dimension_semantics=semantics, vmem_limit_bytes=V7X_VMEM_LIMIT_BYTES)


def _layer_norm(z, g, b):
    mu = jnp.mean(z, axis=-1, keepdims=True)
    d = z - mu
    var = jnp.mean(d * d, axis=-1, keepdims=True)
    return d * lax.rsqrt(var + LN_EPS) * g + b


def _gelu(x):
    return 0.5 * x * (1.0 + lax.erf(x * (1.0 / math.sqrt(2.0))))


def _mm_kernel(x_ref, w_ref, o_ref, *, epilogue):
    acc = jnp.dot(x_ref[...], w_ref[...], preferred_element_type=F32)
    o_ref[...] = epilogue(acc).astype(o_ref.dtype)


def _mm(x, w, *, epilogue, out_dtype, name, tm=1024, tn=1024):
    m, k = x.shape
    n = w.shape[1]
    return pl.pallas_call(
        functools.partial(_mm_kernel, epilogue=epilogue),
        grid=(m // tm, n // tn),
        in_specs=[pl.BlockSpec((tm, k), lambda i, j: (i, 0)),
                  pl.BlockSpec((k, tn), lambda i, j: (0, j))],
        out_specs=pl.BlockSpec((tm, tn), lambda i, j: (i, j)),
        out_shape=jax.ShapeDtypeStruct((m, n), out_dtype),
        compiler_params=_params("parallel", "arbitrary"),
        name=name,
    )(x, w)


def _qkv_kernel(x_ref, w_ref, pos_ref, invf_ref, o_ref, cos_ref, sin_ref, *, tn):
    j = pl.program_id(1)

    @pl.when(j == 0)
    def _():
        ang = pos_ref[...] * invf_ref[...]
        cos_ref[...] = jnp.cos(ang)
        sin_ref[...] = jnp.sin(ang)

    acc = jnp.dot(x_ref[...], w_ref[...], preferred_element_type=F32)
    section = (j * tn) // B_WIDTH
    is_v = (section % 3) == 2

    @pl.when(is_v)
    def _():
        o_ref[...] = acc.astype(o_ref.dtype)

    @pl.when(jnp.logical_not(is_v))
    def _():
        reps = tn // B_HEAD_DIM
        cos = jnp.concatenate([cos_ref[...]] * reps, axis=1)
        sin = jnp.concatenate([sin_ref[...]] * reps, axis=1)
        lane = lax.broadcasted_iota(jnp.int32, acc.shape, 1) & (B_HEAD_DIM - 1)
        half = B_ROT_DIM // 2
        partner = jnp.where(lane < half,
                            pltpu.roll(acc, tn - half, 1),
                            pltpu.roll(acc, half, 1))
        o_ref[...] = (acc * cos + partner * sin).astype(o_ref.dtype)


def _qkv_proj(x, w, posb, invf, *, tm=1024, tn=1024):
    m, k = x.shape
    n = w.shape[1]
    return pl.pallas_call(
        functools.partial(_qkv_kernel, tn=tn),
        grid=(m // tm, n // tn),
        in_specs=[pl.BlockSpec((tm, k), lambda i, j: (i, 0)),
                  pl.BlockSpec((k, tn), lambda i, j: (0, j)),
                  pl.BlockSpec((tm, B_HEAD_DIM), lambda i, j: (i, 0)),
                  pl.BlockSpec((1, B_HEAD_DIM), lambda i, j: (0, 0))],
        out_specs=pl.BlockSpec((tm, tn), lambda i, j: (i, j)),
        out_shape=jax.ShapeDtypeStruct((m, n), BF16),
        scratch_shapes=[pltpu.VMEM((tm, B_HEAD_DIM), F32), pltpu.VMEM((tm, B_HEAD_DIM), F32)],
        compiler_params=_params("arbitrary", "arbitrary"),
        name="qkv_proj",
    )(x, w, posb, invf)


def _attn_kernel(q_ref, kp_ref, kc_ref, vp_ref, vc_ref, o_ref, l_ref):
    n = pl.program_id(2)
    q_blk = B_QBLOCK
    qi = lax.broadcasted_iota(jnp.int32, (q_blk, 2 * q_blk), 0)
    kj = lax.broadcasted_iota(jnp.int32, (q_blk, 2 * q_blk), 1)
    mask = (kj >= qi) & (kj <= qi + q_blk) & ((kj >= q_blk) | (n > 0))
    scale = B_HEAD_DIM ** -0.5
    for h in range(B_HEADS):
        sl = slice(h * B_HEAD_DIM, (h + 1) * B_HEAD_DIM)
        q = q_ref[:, sl]
        k2 = jnp.concatenate([kp_ref[:, sl], kc_ref[:, sl]], axis=0)
        v2 = jnp.concatenate([vp_ref[:, sl], vc_ref[:, sl]], axis=0)
        s = lax.dot_general(q, k2, (((1,), (1,)), ((), ())), preferred_element_type=F32) * scale
        s = jnp.where(mask, s, -jnp.inf)
        mx = jnp.max(s, axis=-1, keepdims=True)
        p = jnp.exp(s - mx)
        den = jnp.sum(p, axis=-1, keepdims=True)
        o = jnp.dot(p.astype(BF16), v2, preferred_element_type=F32) / den
        o_ref[:, sl] = o.astype(o_ref.dtype)
        l_ref[:, sl] = jnp.broadcast_to(mx + jnp.log(den), (q_blk, B_HEAD_DIM))


def _dilated_attention(qkv, batch, seq, group, dilation):
    length = seq // dilation
    nb = length // B_QBLOCK
    n_sections = qkv.shape[1] // B_WIDTH
    view = qkv.reshape(batch, length, dilation * qkv.shape[1])
    base = group * 3

    def spec(which, prev):
        def index(b, r, n):
            blk = jnp.maximum(n - 1, 0) if prev else n
            return (b, blk, r * n_sections + base + which)
        return pl.BlockSpec((None, B_QBLOCK, B_WIDTH), index)

    out_spec = pl.BlockSpec((None, B_QBLOCK, B_WIDTH), lambda b, r, n: (b, n, r))
    o, lse = pl.pallas_call(
        _attn_kernel,
        grid=(batch, dilation, nb),
        in_specs=[spec(0, False), spec(1, True), spec(1, False), spec(2, True), spec(2, False)],
        out_specs=[out_spec, out_spec],
        out_shape=[jax.ShapeDtypeStruct((batch, length, dilation * B_WIDTH), BF16),
                   jax.ShapeDtypeStruct((batch, length, dilation * B_WIDTH), F32)],
        compiler_params=_params("parallel", "parallel", "arbitrary"),
        name=f"dilated_attn_d{dilation}",
    )(view, view, view, view, view)
    return o.reshape(batch * seq, B_WIDTH), lse.reshape(batch * seq, B_WIDTH)


def _combine_kernel(o0_ref, o1_ref, o2_ref, l0_ref, l1_ref, l2_ref, out_ref):
    l0, l1, l2 = l0_ref[...], l1_ref[...], l2_ref[...]
    mx = jnp.maximum(jnp.maximum(l0, l1), l2)
    e0, e1, e2 = jnp.exp(l0 - mx), jnp.exp(l1 - mx), jnp.exp(l2 - mx)
    tot = e0 + e1 + e2
    out = (e0 / tot) * o0_ref[...].astype(F32) + (e1 / tot) * o1_ref[...].astype(F32) \
        + (e2 / tot) * o2_ref[...].astype(F32)
    out_ref[...] = out.astype(out_ref.dtype)


def _combine_groups(outs, lses, *, tm=512):
    m, n = outs[0].shape
    spec = pl.BlockSpec((tm, n), lambda i: (i, 0))
    return pl.pallas_call(
        _combine_kernel,
        grid=(m // tm,),
        in_specs=[spec] * 6,
        out_specs=spec,
        out_shape=jax.ShapeDtypeStruct((m, n), BF16),
        compiler_params=_params("parallel"),
        name="attn_combine",
    )(*outs, *lses)


def _gate_kernel(u_ref, v_ref, g_ref, b_ref, ws_ref, bs_ref, o_ref, *, rows):
    vn = _layer_norm(v_ref[...].astype(F32), g_ref[...], b_ref[...]).astype(BF16)
    ti = lax.broadcasted_iota(jnp.int32, (A_CHUNK, A_CHUNK), 0)
    si = lax.broadcasted_iota(jnp.int32, (A_CHUNK, A_CHUNK), 1)
    causal = si <= ti
    for g in range(A_GROUPS):
        cols = slice(g * A_GROUP_DIM, (g + 1) * A_GROUP_DIM)
        w = jnp.where(causal, ws_ref[g], 0.0).astype(BF16)
        bias = bs_ref[:, cols]
        for c in range(rows // A_CHUNK):
            rws = slice(c * A_CHUNK, (c + 1) * A_CHUNK)
            mixed = jnp.dot(w, vn[rws, cols], preferred_element_type=F32) + bias
            o_ref[rws, cols] = (u_ref[rws, cols].astype(F32) * mixed).astype(o_ref.dtype)


def _spatial_gate(h, ln_g, ln_b, w_s, bs_exp, *, rows=256):
    m = h.shape[0]
    width = D_MODEL
    return pl.pallas_call(
        functools.partial(_gate_kernel, rows=rows),
        grid=(m // rows,),
        in_specs=[pl.BlockSpec((rows, width), lambda i: (i, 0)),
                  pl.BlockSpec((rows, width), lambda i: (i, 1)),
                  pl.BlockSpec((1, width), lambda i: (0, 0)),
                  pl.BlockSpec((1, width), lambda i: (0, 0)),
                  pl.BlockSpec((A_GROUPS, A_CHUNK, A_CHUNK), lambda i: (0, 0, 0)),
                  pl.BlockSpec((A_CHUNK, width), lambda i: (0, 0))],
        out_specs=pl.BlockSpec((rows, width), lambda i: (i, 0)),
        out_shape=jax.ShapeDtypeStruct((m, width), BF16),
        compiler_params=_params("parallel"),
        name="spatial_gate",
    )(h, h, ln_g, ln_b, w_s, bs_exp)


def _pool_kernel(h_ref, halo_ref, wg_ref, scale_ref, o_ref, *, rows, seq):
    i = pl.program_id(0)
    t0 = (i * rows) % seq
    pos = t0 + lax.broadcasted_iota(jnp.int32, (rows, 1), 0)
    keep_halo = t0 > 0
    for g, window in enumerate(C_WINDOWS):
        cols = slice(g * C_GROUP_DIM, (g + 1) * C_GROUP_DIM)
        hg = h_ref[:, cols]
        halo = jnp.where(keep_halo, halo_ref[:, cols], 0.0)
        acc = jnp.concatenate([halo, hg], axis=0)
        span = 1
        while span < window:
            acc = acc + pltpu.roll(acc, span, 0)
            span *= 2
        den = jnp.minimum(pos + 1, window).astype(F32)
        pooled = acc[C_HALO:, :] / den - hg
        y = jnp.dot(pooled.astype(BF16), wg_ref[g], preferred_element_type=F32) * scale_ref[:, cols]
        o_ref[:, cols] = y.astype(o_ref.dtype)


def _pool_mix(h, w_group, scale, seq, *, rows=256):
    m, width = h.shape
    halo_per_block = rows // C_HALO
    return pl.pallas_call(
        functools.partial(_pool_kernel, rows=rows, seq=seq),
        grid=(m // rows,),
        in_specs=[pl.BlockSpec((rows, width), lambda i: (i, 0)),
                  pl.BlockSpec((C_HALO, width), lambda i: (jnp.maximum(i * halo_per_block - 1, 0), 0)),
                  pl.BlockSpec((len(C_WINDOWS), C_GROUP_DIM, C_GROUP_DIM), lambda i: (0, 0, 0)),
                  pl.BlockSpec((1, width), lambda i: (0, 0))],
        out_specs=pl.BlockSpec((rows, width), lambda i: (i, 0)),
        out_shape=jax.ShapeDtypeStruct((m, width), BF16),
        compiler_params=_params("parallel"),
        name="pool_mix",
    )(h, h, w_group, scale)


def _ffn_up_kernel(x_ref, w_ref, cw_ref, cb_ref, o_ref, hbuf, carry, *, tm, tiles_per_seq):
    i = pl.program_id(0)
    j = pl.program_id(1)
    h = jnp.dot(x_ref[...], w_ref[...], preferred_element_type=F32)
    seq_start = (i % tiles_per_seq) == 0

    @pl.when(seq_start)
    def _():
        hbuf[0:CARRY_ROWS, :] = jnp.zeros((CARRY_ROWS, hbuf.shape[1]), F32)

    @pl.when(jnp.logical_not(seq_start))
    def _():
        hbuf[0:CARRY_ROWS, :] = carry[j]

    hbuf[CARRY_ROWS:, :] = h
    carry[j] = h[tm - CARRY_ROWS:, :]
    h1 = hbuf[CARRY_ROWS - 1:CARRY_ROWS - 1 + tm, :]
    h2 = hbuf[CARRY_ROWS - 2:CARRY_ROWS - 2 + tm, :]
    y = cw_ref[2:3, :] * h + cw_ref[1:2, :] * h1 + cw_ref[0:1, :] * h2 + cb_ref[...]
    gate = y[:, :FF_BLOCK]
    up = y[:, FF_BLOCK:]
    o_ref[...] = (gate / (1.0 + jnp.exp(-gate)) * up).astype(o_ref.dtype)


def _ffn_up(x, w, conv_w, conv_b, seq, *, tm=1024):
    m, k = x.shape
    nj = w.shape[1] // (2 * FF_BLOCK)
    return pl.pallas_call(
        functools.partial(_ffn_up_kernel, tm=tm, tiles_per_seq=seq // tm),
        grid=(m // tm, nj),
        in_specs=[pl.BlockSpec((tm, k), lambda i, j: (i, 0)),
                  pl.BlockSpec((k, 2 * FF_BLOCK), lambda i, j: (0, j)),
                  pl.BlockSpec((CONV_WIDTH, 2 * FF_BLOCK), lambda i, j: (0, j)),
                  pl.BlockSpec((1, 2 * FF_BLOCK), lambda i, j: (0, j))],
        out_specs=pl.BlockSpec((tm, FF_BLOCK), lambda i, j: (i, j)),
        out_shape=jax.ShapeDtypeStruct((m, nj * FF_BLOCK), BF16),
        scratch_shapes=[pltpu.VMEM((tm + CARRY_ROWS, 2 * FF_BLOCK), F32),
                        pltpu.VMEM((nj, CARRY_ROWS, 2 * FF_BLOCK), F32)],
        compiler_params=_params("arbitrary", "arbitrary"),
        name="ffn_up_conv_gate",
    )(x, w, conv_w, conv_b)


def _proj_ln_kernel(a_ref, w_ref, x_ref, g_ref, b_ref, of_ref, ob_ref, *, nk, n_chunks):
    k = pl.program_id(1)

    @pl.when(k == 0)
    def _():
        of_ref[...] = jnp.zeros(of_ref.shape, F32)

    of_ref[...] += jnp.dot(a_ref[...], w_ref[...], preferred_element_type=F32)

    cw = of_ref.shape[1] // n_chunks
    for c in range(n_chunks):
        @pl.when(k == c)
        def _(c=c):
            of_ref[:, c * cw:(c + 1) * cw] += DEEPNORM_ALPHA * x_ref[...]

    @pl.when(k == nk - 1)
    def _():
        def norm_rows(r, carry):
            rows = pl.ds(pl.multiple_of(r * LN_ROWS, LN_ROWS), LN_ROWS)
            y = _layer_norm(of_ref[rows, :], g_ref[...], b_ref[...])
            of_ref[rows, :] = y
            ob_ref[rows, :] = y.astype(ob_ref.dtype)
            return carry
        lax.fori_loop(0, of_ref.shape[0] // LN_ROWS, norm_rows, 0)


def _proj_ln(a, w, x, g, b, *, name, tm=512, tk=1024):
    m, kdim = a.shape
    n = w.shape[1]
    nk = kdim // tk
    n_chunks = 1
    while n_chunks * 2 <= min(nk, 8):
        n_chunks *= 2
    cw = n // n_chunks
    return pl.pallas_call(
        functools.partial(_proj_ln_kernel, nk=nk, n_chunks=n_chunks),
        grid=(m // tm, nk),
        in_specs=[pl.BlockSpec((tm, tk), lambda i, k: (i, k)),
                  pl.BlockSpec((tk, n), lambda i, k: (k, 0)),
                  pl.BlockSpec((tm, cw), lambda i, k: (i, jnp.minimum(k, n_chunks - 1))),
                  pl.BlockSpec((1, n), lambda i, k: (0, 0)),
                  pl.BlockSpec((1, n), lambda i, k: (0, 0))],
        out_specs=[pl.BlockSpec((tm, n), lambda i, k: (i, 0)),
                   pl.BlockSpec((tm, n), lambda i, k: (i, 0))],
        out_shape=[jax.ShapeDtypeStruct((m, n), F32), jax.ShapeDtypeStruct((m, n), BF16)],
        compiler_params=_params("parallel", "arbitrary"),
        name=name,
    )(a, w, x, g.reshape(1, n), b.reshape(1, n))


def _interleave_gate_up(t):
    lead = t.shape[:-1]
    pad = [(0, 0)] * len(lead) + [(0, D_FF_PAD - D_FF)]
    gate = jnp.pad(t[..., :D_FF], pad).reshape(lead + (-1, 1, FF_BLOCK))
    up = jnp.pad(t[..., D_FF:], pad).reshape(lead + (-1, 1, FF_BLOCK))
    return jnp.concatenate([gate, up], axis=-2).reshape(lead + (2 * D_FF_PAD,))


def _rotary_inv_freq():
    half = B_ROT_DIM // 2
    inv_freq = jnp.float32(ROPE_THETA) ** (-jnp.arange(half, dtype=F32) * 2.0 / B_ROT_DIM)
    zeros = jnp.zeros((B_HEAD_DIM - B_ROT_DIM,), F32)
    return jnp.concatenate([-inv_freq, inv_freq, zeros]).reshape(1, B_HEAD_DIM)


def kernel(x, positions, a_w_in, a_ln_g, a_ln_b, a_w_s, a_b_s, a_w_out, b_w_in, b_w_out, c_w_in, c_w_group, c_scale, c_w_out, f_w_up, f_conv_w, f_conv_b, f_w_down, ln_mix_g, ln_mix_b, ln_ffn_g, ln_ffn_b):
    batch, seq, d = x.shape
    m = batch * seq
    xf = x.reshape(m, d)
    xb = xf.astype(BF16)
    posb = jnp.broadcast_to(positions.reshape(m, 1).astype(F32), (m, B_HEAD_DIM))
    invf = _rotary_inv_freq()

    for layer in range(DEPTH):
        kind, idx = layer % N_MIXERS, layer // N_MIXERS
        if kind == 0:
            h = _mm(xb, a_w_in[idx].astype(BF16), epilogue=_gelu, out_dtype=BF16, name="gmlp_in_gelu")
            bs_exp = jnp.repeat(a_b_s[idx].T, A_GROUP_DIM, axis=1)
            y = _spatial_gate(h, a_ln_g[idx].reshape(1, d), a_ln_b[idx].reshape(1, d), a_w_s[idx], bs_exp)
            w_out = a_w_out[idx]
        elif kind == 1:
            qkv = _qkv_proj(xb, b_w_in[idx].astype(BF16), posb, invf)
            outs, lses = [], []
            for group, (window, dilation) in enumerate(B_PATTERNS):
                assert window // dilation == B_QBLOCK
                o, lse = _dilated_attention(qkv, batch, seq, group, dilation)
                outs.append(o)
                lses.append(lse)
            y = _combine_groups(outs, lses)
            w_out = b_w_out[idx]
        else:
            h = _mm(xb, c_w_in[idx].astype(BF16), epilogue=lambda t: t, out_dtype=F32, name="pool_in")
            y = _pool_mix(h, c_w_group[idx].astype(BF16), c_scale[idx].reshape(1, d), seq)
            w_out = c_w_out[idx]
        xf, xb = _proj_ln(y, w_out.astype(BF16), xf, ln_mix_g[layer], ln_mix_b[layer], name="mixer_out_ln")

        w_up = _interleave_gate_up(f_w_up[layer]).astype(BF16)
        conv_w = _interleave_gate_up(f_conv_w[layer])
        conv_b = _interleave_gate_up(f_conv_b[layer]).reshape(1, -1)
        w_down = jnp.pad(f_w_down[layer], ((0, D_FF_PAD - D_FF), (0, 0))).astype(BF16)
        g = _ffn_up(xb, w_up, conv_w, conv_b, seq)
        xf, xb = _proj_ln(g, w_down, xf, ln_ffn_g[layer], ln_ffn_b[layer], name="ffn_down_ln")

    return xf.reshape(batch, seq, d)
```

```python
import functools
import math

import jax
import jax.numpy as jnp
from jax import lax
from jax.experimental import pallas as pl
from jax.experimental.pallas import tpu as pltpu

F32 = jnp.float32
BF16 = jnp.bfloat16

D_MODEL = 4096
DEPTH = 4
N_MIXERS = 3

A_CHUNK = 128
A_GROUPS = 16
A_GROUP_DIM = D_MODEL // A_GROUPS

B_PATTERNS = ((128, 1), (512, 4), (2048, 16))
B_HEADS = 16
B_HEAD_DIM = 128
B_ROT_DIM = B_HEAD_DIM // 4
B_QBLOCK = 128
B_WIDTH = B_HEADS * B_HEAD_DIM
ROPE_THETA = 500000.0

C_WINDOWS = (2, 4, 8, 16)
C_GROUP_DIM = D_MODEL // len(C_WINDOWS)
C_HALO = 16

D_FF = 11008
FF_BLOCK = 512
CONV_WIDTH = 3
CARRY_ROWS = 8

LN_EPS = 1e-5
LN_ROWS = 8
DEEPNORM_ALPHA = (2 * DEPTH) ** 0.25

LANES = 128
V7X_VMEM_LIMIT_BYTES = 56 * 1024 * 1024


def _params(*semantics):
    return pltpu.CompilerParams(dimension_semantics=semantics, vmem_limit_bytes=V7X_VMEM_LIMIT_BYTES)


def _layer_norm(z, g, b):
    mu = jnp.mean(z, axis=-1, keepdims=True)
    d = z - mu
    var = jnp.mean(d * d, axis=-1, keepdims=True)
    return d * lax.rsqrt(var + LN_EPS) * g + b


def _gelu(x):
    return 0.5 * x * (1.0 + lax.erf(x * (1.0 / math.sqrt(2.0))))


def _mm_kernel(x_ref, w_ref, o_ref, *, epilogue):
    acc = jnp.dot(x_ref[...], w_ref[...], preferred_element_type=F32)
    o_ref[...] = epilogue(acc).astype(o_ref.dtype)


def _mm(x, w, *, epilogue, out_dtype, name, tm=1024, tn=1024):
    m, k = x.shape
    n = w.shape[1]
    return pl.pallas_call(
        functools.partial(_mm_kernel, epilogue=epilogue),
        grid=(m // tm, n // tn),
        in_specs=[pl.BlockSpec((tm, k), lambda i, j: (i, 0)),
                  pl.BlockSpec((k, tn), lambda i, j: (0, j))],
        out_specs=pl.BlockSpec((tm, tn), lambda i, j: (i, j)),
        out_shape=jax.ShapeDtypeStruct((m, n), out_dtype),
        compiler_params=_params("parallel", "arbitrary"),
        name=name,
    )(x, w)


def _qkv_kernel(x_ref, w_ref, pos_ref, invf_ref, o_ref, cos_ref, sin_ref, *perm_scratch,
                tm, tn, dilation):
    j = pl.program_id(1)

    @pl.when(j == 0)
    def _():
        ang = pos_ref[...] * invf_ref[...]
        cos_ref[...] = jnp.cos(ang)
        sin_ref[...] = jnp.sin(ang)

    acc = jnp.dot(x_ref[...], w_ref[...], preferred_element_type=F32)
    rot = jnp.where((j * tn) // B_WIDTH < 2, 1.0, 0.0).astype(F32)
    cos1 = 1.0 + (cos_ref[...] - 1.0) * rot
    sin1 = sin_ref[...] * rot
    reps = tn // B_HEAD_DIM
    cos = jnp.concatenate([cos1] * reps, axis=1)
    sin = jnp.concatenate([sin1] * reps, axis=1)
    lane = lax.broadcasted_iota(jnp.int32, acc.shape, 1) & (B_HEAD_DIM - 1)
    half = B_ROT_DIM // 2
    partner = jnp.where(lane < half,
                        pltpu.roll(acc, tn - half, 1),
                        pltpu.roll(acc, half, 1))
    out = acc * cos + partner * sin

    chunk = B_QBLOCK * dilation
    if dilation == 1:
        for c in range(tm // chunk):
            o_ref[c, 0] = out[c * chunk:(c + 1) * chunk].astype(o_ref.dtype)
        return
    (buf,) = perm_scratch
    for t in range(tn // LANES):
        cols = slice(t * LANES, (t + 1) * LANES)
        buf[t] = out[:, cols]
        if chunk <= tm:
            for c in range(tm // chunk):
                for r in range(dilation):
                    rows = pl.ds(c * chunk + r, B_QBLOCK, stride=dilation)
                    o_ref[c, r, :, cols] = buf[t, rows, :].astype(o_ref.dtype)
        else:
            for r in range(dilation):
                rows = pl.ds(r, tm // dilation, stride=dilation)
                o_ref[r, :, cols] = buf[t, rows, :].astype(o_ref.dtype)


def _qkv_proj(x, w, posb, invf, dilation, *, tm=1024, tn=1024):
    m, k = x.shape
    n = w.shape[1]
    chunk = B_QBLOCK * dilation
    if chunk <= tm:
        out_spec = pl.BlockSpec((tm // chunk, dilation, B_QBLOCK, tn), lambda i, j: (i, 0, 0, j))
    else:
        tiles_per_chunk = chunk // tm
        out_spec = pl.BlockSpec((None, dilation, tm // dilation, tn),
                                lambda i, j: (i // tiles_per_chunk, 0, i % tiles_per_chunk, j))
    scratch = [pltpu.VMEM((tm, B_HEAD_DIM), F32), pltpu.VMEM((tm, B_HEAD_DIM), F32)]
    if dilation > 1:
        scratch.append(pltpu.VMEM((tn // LANES, tm, LANES), F32))
    return pl.pallas_call(
        functools.partial(_qkv_kernel, tm=tm, tn=tn, dilation=dilation),
        grid=(m // tm, n // tn),
        in_specs=[pl.BlockSpec((tm, k), lambda i, j: (i, 0)),
                  pl.BlockSpec((k, tn), lambda i, j: (0, j)),
                  pl.BlockSpec((tm, B_HEAD_DIM), lambda i, j: (i, 0)),
                  pl.BlockSpec((1, B_HEAD_DIM), lambda i, j: (0, 0))],
        out_specs=out_spec,
        out_shape=jax.ShapeDtypeStruct((m // chunk, dilation, B_QBLOCK, n), BF16),
        scratch_shapes=scratch,
        compiler_params=_params("arbitrary", "arbitrary"),
        name=f"qkv_proj_d{dilation}",
    )(x, w, posb, invf)


def _attn_kernel(q_ref, kp_ref, kc_ref, vp_ref, vc_ref, o_ref, l_ref, *, chunks_per_seq):
    c = pl.program_id(0)
    has_prev = (c % chunks_per_seq) > 0
    q_blk = B_QBLOCK
    qi = lax.broadcasted_iota(jnp.int32, (q_blk, 2 * q_blk), 0)
    kj = lax.broadcasted_iota(jnp.int32, (q_blk, 2 * q_blk), 1)
    mask = (kj >= qi) & (kj <= qi + q_blk) & ((kj >= q_blk) | has_prev)
    scale = B_HEAD_DIM ** -0.5
    head_lane = lax.broadcasted_iota(jnp.int32, (q_blk, LANES), 1)
    lse_tile = jnp.zeros((q_blk, LANES), F32)
    for h in range(B_HEADS):
        sl = slice(h * B_HEAD_DIM, (h + 1) * B_HEAD_DIM)
        q = q_ref[:, sl]
        k2 = jnp.concatenate([kp_ref[:, sl], kc_ref[:, sl]], axis=0)
        v2 = jnp.concatenate([vp_ref[:, sl], vc_ref[:, sl]], axis=0)
        s = lax.dot_general(q, k2, (((1,), (1,)), ((), ())), preferred_element_type=F32) * scale
        s = jnp.where(mask, s, -jnp.inf)
        mx = jnp.max(s, axis=-1, keepdims=True)
        p = jnp.exp(s - mx)
        den = jnp.sum(p, axis=-1, keepdims=True)
        o = jnp.dot(p.astype(BF16), v2, preferred_element_type=F32) / den
        o_ref[:, sl] = o.astype(o_ref.dtype)
        lse_tile = jnp.where(head_lane == h, mx + jnp.log(den), lse_tile)
    l_ref[...] = lse_tile


def _dilated_attention(qkv, seq, dilation):
    n_chunks = qkv.shape[0]
    chunks_per_seq = seq // (B_QBLOCK * dilation)

    def spec(which, prev):
        def index(c, r):
            return (jnp.maximum(c - 1, 0) if prev else c, r, 0, which)
        return pl.BlockSpec((None, None, B_QBLOCK, B_WIDTH), index)

    return pl.pallas_call(
        functools.partial(_attn_kernel, chunks_per_seq=chunks_per_seq),
        grid=(n_chunks, dilation),
        in_specs=[spec(0, False), spec(1, True), spec(1, False), spec(2, True), spec(2, False)],
        out_specs=[pl.BlockSpec((None, None, B_QBLOCK, B_WIDTH), lambda c, r: (c, r, 0, 0)),
                   pl.BlockSpec((None, None, B_QBLOCK, LANES), lambda c, r: (c, r, 0, 0))],
        out_shape=[jax.ShapeDtypeStruct((n_chunks, dilation, B_QBLOCK, B_WIDTH), BF16),
                   jax.ShapeDtypeStruct((n_chunks, dilation, B_QBLOCK, LANES), F32)],
        compiler_params=_params("parallel", "parallel"),
        name=f"dilated_attn_d{dilation}",
    )(qkv, qkv, qkv, qkv, qkv)


def _combine_kernel(*refs, tm, dilations):
    n = len(dilations)
    o_refs, l_refs, out_ref = refs[:n], refs[n:2 * n], refs[2 * n]
    o_bufs, l_bufs = refs[2 * n + 1:3 * n + 1], refs[3 * n + 1:4 * n + 1]

    def scatter(src_o, src_l, o_buf, l_buf, rows):
        l_buf[rows, :] = src_l
        for h in range(B_HEADS):
            o_buf[h, rows, :] = src_o[:, h * B_HEAD_DIM:(h + 1) * B_HEAD_DIM].astype(F32)

    for o_ref, l_ref, o_buf, l_buf, d in zip(o_refs, l_refs, o_bufs, l_bufs, dilations):
        chunk = B_QBLOCK * d
        if chunk <= tm:
            for c in range(tm // chunk):
                for r in range(d):
                    rows = pl.ds(c * chunk + r, B_QBLOCK, stride=d) if d > 1 else pl.ds(c * chunk, B_QBLOCK)
                    scatter(o_ref[c, r], l_ref[c, r], o_buf, l_buf, rows)
        else:
            for r in range(d):
                scatter(o_ref[r], l_ref[r], o_buf, l_buf, pl.ds(r, tm // d, stride=d))

    lses = [l_buf[...] for l_buf in l_bufs]
    mx = functools.reduce(jnp.maximum, lses)
    es = [jnp.exp(l - mx) for l in lses]
    tot = functools.reduce(lambda a, b: a + b, es)
    ws = [e / tot for e in es]
    for h in range(B_HEADS):
        acc = None
        for w, o_buf in zip(ws, o_bufs):
            term = jnp.broadcast_to(w[:, h:h + 1], (tm, B_HEAD_DIM)) * o_buf[h]
            acc = term if acc is None else acc + term
        out_ref[:, h * B_HEAD_DIM:(h + 1) * B_HEAD_DIM] = acc.astype(out_ref.dtype)


def _combine_groups(outs, lses, dilations, m, *, tm=512):
    def spec(d, width):
        chunk = B_QBLOCK * d
        if chunk <= tm:
            return pl.BlockSpec((tm // chunk, d, B_QBLOCK, width), lambda i: (i, 0, 0, 0))
        tiles_per_chunk = chunk // tm
        return pl.BlockSpec((None, d, tm // d, width),
                            lambda i: (i // tiles_per_chunk, 0, i % tiles_per_chunk, 0))

    n = len(dilations)
    return pl.pallas_call(
        functools.partial(_combine_kernel, tm=tm, dilations=dilations),
        grid=(m // tm,),
        in_specs=[spec(d, B_WIDTH) for d in dilations] + [spec(d, LANES) for d in dilations],
        out_specs=pl.BlockSpec((tm, B_WIDTH), lambda i: (i, 0)),
        out_shape=jax.ShapeDtypeStruct((m, B_WIDTH), BF16),
        scratch_shapes=[pltpu.VMEM((B_HEADS, tm, B_HEAD_DIM), F32)] * n + [pltpu.VMEM((tm, LANES), F32)] * n,
        compiler_params=_params("parallel"),
        name="attn_combine",
    )(*outs, *lses)


def _gate_kernel(u_ref, v_ref, g_ref, b_ref, ws_ref, bs_ref, o_ref, *, rows):
    vn = _layer_norm(v_ref[...].astype(F32), g_ref[...], b_ref[...]).astype(BF16)
    ti = lax.broadcasted_iota(jnp.int32, (A_CHUNK, A_CHUNK), 0)
    si = lax.broadcasted_iota(jnp.int32, (A_CHUNK, A_CHUNK), 1)
    causal = si <= ti
    for g in range(A_GROUPS):
        cols = slice(g * A_GROUP_DIM, (g + 1) * A_GROUP_DIM)
        w = jnp.where(causal, ws_ref[g], 0.0).astype(BF16)
        bias = bs_ref[:, cols]
        for c in range(rows // A_CHUNK):
            rws = slice(c * A_CHUNK, (c + 1) * A_CHUNK)
            mixed = jnp.dot(w, vn[rws, cols], preferred_element_type=F32) + bias
            o_ref[rws, cols] = (u_ref[rws, cols].astype(F32) * mixed).astype(o_ref.dtype)


def _spatial_gate(h, ln_g, ln_b, w_s, bs_exp, *, rows=256):
    m = h.shape[0]
    width = D_MODEL
    return pl.pallas_call(
        functools.partial(_gate_kernel, rows=rows),
        grid=(m // rows,),
        in_specs=[pl.BlockSpec((rows, width), lambda i: (i, 0)),
                  pl.BlockSpec((rows, width), lambda i: (i, 1)),
                  pl.BlockSpec((1, width), lambda i: (0, 0)),
                  pl.BlockSpec((1, width), lambda i: (0, 0)),
                  pl.BlockSpec((A_GROUPS, A_CHUNK, A_CHUNK), lambda i: (0, 0, 0)),
                  pl.BlockSpec((A_CHUNK, width), lambda i: (0, 0))],
        out_specs=pl.BlockSpec((rows, width), lambda i: (i, 0)),
        out_shape=jax.ShapeDtypeStruct((m, width), BF16),
        compiler_params=_params("parallel"),
        name="spatial_gate",
    )(h, h, ln_g, ln_b, w_s, bs_exp)


def _pool_kernel(h_ref, halo_ref, wg_ref, scale_ref, o_ref, *, rows, seq):
    i = pl.program_id(0)
    t0 = (i * rows) % seq
    pos = t0 + lax.broadcasted_iota(jnp.int32, (rows, 1), 0)
    keep_halo = t0 > 0
    for g, window in enumerate(C_WINDOWS):
        cols = slice(g * C_GROUP_DIM, (g + 1) * C_GROUP_DIM)
        hg = h_ref[:, cols]
        halo = jnp.where(keep_halo, halo_ref[:, cols], 0.0)
        acc = jnp.concatenate([halo, hg], axis=0)
        span = 1
        while span < window:
            acc = acc + pltpu.roll(acc, span, 0)
            span *= 2
        den = jnp.minimum(pos + 1, window).astype(F32)
        pooled = acc[C_HALO:, :] / den - hg
        y = jnp.dot(pooled.astype(BF16), wg_ref[g], preferred_element_type=F32) * scale_ref[:, cols]
        o_ref[:, cols] = y.astype(o_ref.dtype)


def _pool_mix(h, w_group, scale, seq, *, rows=256):
    m, width = h.shape
    halo_per_block = rows // C_HALO
    return pl.pallas_call(
        functools.partial(_pool_kernel, rows=rows, seq=seq),
        grid=(m // rows,),
        in_specs=[pl.BlockSpec((rows, width), lambda i: (i, 0)),
                  pl.BlockSpec((C_HALO, width), lambda i: (jnp.maximum(i * halo_per_block - 1, 0), 0)),
                  pl.BlockSpec((len(C_WINDOWS), C_GROUP_DIM, C_GROUP_DIM), lambda i: (0, 0, 0)),
                  pl.BlockSpec((1, width), lambda i: (0, 0))],
        out_specs=pl.BlockSpec((rows, width), lambda i: (i, 0)),
        out_shape=jax.ShapeDtypeStruct((m, width), BF16),
        compiler_params=_params("parallel"),
        name="pool_mix",
    )(h, h, w_group, scale)


def _ffn_up_kernel(x_ref, wg_ref, wu_ref, cwg_ref, cwu_ref, cbg_ref, cbu_ref, o_ref, hbuf, carry,
                   *, tm, tiles_per_seq):
    i = pl.program_id(0)
    j = pl.program_id(1)
    seq_start = (i % tiles_per_seq) == 0

    @pl.when(seq_start)
    def _():
        hbuf[:, 0:CARRY_ROWS, :] = jnp.zeros((2, CARRY_ROWS, FF_BLOCK), F32)

    @pl.when(jnp.logical_not(seq_start))
    def _():
        hbuf[:, 0:CARRY_ROWS, :] = carry[j]

    x = x_ref[...]
    halves = []
    for s, (w_ref, cw_ref, cb_ref) in enumerate(((wg_ref, cwg_ref, cbg_ref), (wu_ref, cwu_ref, cbu_ref))):
        h = jnp.dot(x, w_ref[...], preferred_element_type=F32)
        hbuf[s, CARRY_ROWS:, :] = h
        carry[j, s] = h[tm - CARRY_ROWS:, :]
        h1 = hbuf[s, CARRY_ROWS - 1:CARRY_ROWS - 1 + tm, :]
        h2 = hbuf[s, CARRY_ROWS - 2:CARRY_ROWS - 2 + tm, :]
        halves.append(cw_ref[2:3, :] * h + cw_ref[1:2, :] * h1 + cw_ref[0:1, :] * h2 + cb_ref[...])
    gate, up = halves
    o_ref[...] = (gate / (1.0 + jnp.exp(-gate)) * up).astype(o_ref.dtype)


def _ffn_up(x, w_gate, w_up, cw_gate, cw_up, cb_gate, cb_up, seq, *, tm=1024):
    m, k = x.shape
    d_ff = w_gate.shape[1]
    nj = pl.cdiv(d_ff, FF_BLOCK)
    col = lambda i, j: (0, j)
    return pl.pallas_call(
        functools.partial(_ffn_up_kernel, tm=tm, tiles_per_seq=seq // tm),
        grid=(m // tm, nj),
        in_specs=[pl.BlockSpec((tm, k), lambda i, j: (i, 0)),
                  pl.BlockSpec((k, FF_BLOCK), col), pl.BlockSpec((k, FF_BLOCK), col),
                  pl.BlockSpec((CONV_WIDTH, FF_BLOCK), col), pl.BlockSpec((CONV_WIDTH, FF_BLOCK), col),
                  pl.BlockSpec((1, FF_BLOCK), col), pl.BlockSpec((1, FF_BLOCK), col)],
        out_specs=pl.BlockSpec((tm, FF_BLOCK), lambda i, j: (i, j)),
        out_shape=jax.ShapeDtypeStruct((m, d_ff), BF16),
        scratch_shapes=[pltpu.VMEM((2, tm + CARRY_ROWS, FF_BLOCK), F32),
                        pltpu.VMEM((nj, 2, CARRY_ROWS, FF_BLOCK), F32)],
        compiler_params=_params("arbitrary", "arbitrary"),
        name="ffn_up_conv_gate",
    )(x, w_gate, w_up, cw_gate, cw_up, cb_gate, cb_up)


def _proj_ln_kernel(a_ref, w_ref, x_ref, g_ref, b_ref, of_ref, ob_ref, mu_ref, rstd_ref,
                    *, nk, k_tail, n_chunks):
    k = pl.program_id(1)
    tm, n = of_ref.shape

    @pl.when(k == 0)
    def _():
        of_ref[...] = jnp.dot(a_ref[...], w_ref[...], preferred_element_type=F32)

    if k_tail == a_ref.shape[1]:
        @pl.when(k > 0)
        def _():
            of_ref[...] += jnp.dot(a_ref[...], w_ref[...], preferred_element_type=F32)
    else:
        @pl.when((k > 0) & (k < nk - 1))
        def _():
            of_ref[...] += jnp.dot(a_ref[...], w_ref[...], preferred_element_type=F32)

        @pl.when(k == nk - 1)
        def _():
            of_ref[...] += jnp.dot(a_ref[:, :k_tail], w_ref[:k_tail, :], preferred_element_type=F32)

    cw = n // n_chunks
    for c in range(n_chunks):
        @pl.when(k == c)
        def _(c=c):
            of_ref[:, c * cw:(c + 1) * cw] += DEEPNORM_ALPHA * x_ref[...]

    @pl.when(k == nk - 1)
    def _():
        steps = tm // LN_ROWS

        def rows_at(r):
            return pl.ds(pl.multiple_of(r * LN_ROWS, LN_ROWS), LN_ROWS)

        def mean_pass(r, carry):
            z = of_ref[rows_at(r), :]
            mu_ref[rows_at(r), :] = jnp.broadcast_to(jnp.mean(z, axis=-1, keepdims=True), (LN_ROWS, LANES))
            return carry

        def across(stat_ref, r):
            return jnp.concatenate([stat_ref[rows_at(r), :]] * (n // LANES), axis=1)

        def var_pass(r, carry):
            d = of_ref[rows_at(r), :] - across(mu_ref, r)
            var = jnp.mean(d * d, axis=-1, keepdims=True)
            rstd_ref[rows_at(r), :] = jnp.broadcast_to(lax.rsqrt(var + LN_EPS), (LN_ROWS, LANES))
            return carry

        def norm_pass(r, carry):
            d = of_ref[rows_at(r), :] - across(mu_ref, r)
            y = d * across(rstd_ref, r) * g_ref[...] + b_ref[...]
            of_ref[rows_at(r), :] = y
            ob_ref[rows_at(r), :] = y.astype(ob_ref.dtype)
            return carry

        lax.fori_loop(0, steps, mean_pass, 0, unroll=16)
        lax.fori_loop(0, steps, var_pass, 0, unroll=16)
        lax.fori_loop(0, steps, norm_pass, 0, unroll=2)


def _proj_ln(a, w, x, g, b, *, name, tm=512, tk=1024):
    m, kdim = a.shape
    n = w.shape[1]
    nk = pl.cdiv(kdim, tk)
    k_tail = kdim - (nk - 1) * tk
    n_chunks = 1
    while n_chunks * 2 <= min(nk, 8):
        n_chunks *= 2
    cw = n // n_chunks
    return pl.pallas_call(
        functools.partial(_proj_ln_kernel, nk=nk, k_tail=k_tail, n_chunks=n_chunks),
        grid=(m // tm, nk),
        in_specs=[pl.BlockSpec((tm, tk), lambda i, k: (i, k)),
                  pl.BlockSpec((tk, n), lambda i, k: (k, 0)),
                  pl.BlockSpec((tm, cw), lambda i, k: (i, jnp.minimum(k, n_chunks - 1))),
                  pl.BlockSpec((LN_ROWS, n), lambda i, k: (0, 0)),
                  pl.BlockSpec((LN_ROWS, n), lambda i, k: (0, 0))],
        out_specs=[pl.BlockSpec((tm, n), lambda i, k: (i, 0)),
                   pl.BlockSpec((tm, n), lambda i, k: (i, 0))],
        out_shape=[jax.ShapeDtypeStruct((m, n), F32), jax.ShapeDtypeStruct((m, n), BF16)],
        scratch_shapes=[pltpu.VMEM((tm, LANES), F32), pltpu.VMEM((tm, LANES), F32)],
        compiler_params=_params("parallel", "arbitrary"),
        name=name,
    )(a, w, x, jnp.broadcast_to(g.reshape(1, n), (LN_ROWS, n)), jnp.broadcast_to(b.reshape(1, n), (LN_ROWS, n)))


def _rotary_inv_freq():
    half = B_ROT_DIM // 2
    inv_freq = jnp.float32(ROPE_THETA) ** (-jnp.arange(half, dtype=F32) * 2.0 / B_ROT_DIM)
    zeros = jnp.zeros((B_HEAD_DIM - B_ROT_DIM,), F32)
    return jnp.concatenate([-inv_freq, inv_freq, zeros]).reshape(1, B_HEAD_DIM)


def kernel(x, positions, a_w_in, a_ln_g, a_ln_b, a_w_s, a_b_s, a_w_out, b_w_in, b_w_out, c_w_in, c_w_group, c_scale, c_w_out, f_w_up, f_conv_w, f_conv_b, f_w_down, ln_mix_g, ln_mix_b, ln_ffn_g, ln_ffn_b):
    batch, seq, d = x.shape
    m = batch * seq
    xf = x.reshape(m, d)
    xb = xf.astype(BF16)
    posb = jnp.broadcast_to(positions.reshape(m, 1).astype(F32), (m, B_HEAD_DIM))
    invf = _rotary_inv_freq()

    for layer in range(DEPTH):
        kind, idx = layer % N_MIXERS, layer // N_MIXERS
        if kind == 0:
            h = _mm(xb, a_w_in[idx].astype(BF16), epilogue=_gelu, out_dtype=BF16, name="gmlp_in_gelu")
            bs_exp = jnp.repeat(a_b_s[idx].T, A_GROUP_DIM, axis=1)
            y = _spatial_gate(h, a_ln_g[idx].reshape(1, d), a_ln_b[idx].reshape(1, d), a_w_s[idx], bs_exp)
            w_out = a_w_out[idx]
        elif kind == 1:
            outs, lses, dilations = [], [], []
            group_cols = 3 * B_WIDTH
            for group, (window, dilation) in enumerate(B_PATTERNS):
                assert window // dilation == B_QBLOCK
                w_g = b_w_in[idx][:, group * group_cols:(group + 1) * group_cols].astype(BF16)
                qkv = _qkv_proj(xb, w_g, posb, invf, dilation)
                o, lse = _dilated_attention(qkv, seq, dilation)
                outs.append(o)
                lses.append(lse)
                dilations.append(dilation)
            y = _combine_groups(outs, lses, tuple(dilations), m)
            w_out = b_w_out[idx]
        else:
            h = _mm(xb, c_w_in[idx].astype(BF16), epilogue=lambda t: t, out_dtype=F32, name="pool_in")
            y = _pool_mix(h, c_w_group[idx].astype(BF16), c_scale[idx].reshape(1, d), seq)
            w_out = c_w_out[idx]
        xf, xb = _proj_ln(y, w_out.astype(BF16), xf, ln_mix_g[layer], ln_mix_b[layer], name="mixer_out_ln")

        w_up = f_w_up[layer]
        cw, cb = f_conv_w[layer], f_conv_b[layer].reshape(1, -1)
        g = _ffn_up(xb, w_up[:, :D_FF].astype(BF16), w_up[:, D_FF:].astype(BF16),
                    cw[:, :D_FF], cw[:, D_FF:], cb[:, :D_FF], cb[:, D_FF:], seq)
        xf, xb = _proj_ln(g, f_w_down[layer].astype(BF16), xf, ln_ffn_g[layer], ln_ffn_b[layer],
                          name="ffn_down_ln")

    return xf.reshape(batch, seq, d)
```

```python
import functools
import math

import jax
import jax.numpy as jnp
from jax import lax
from jax.experimental import pallas as pl
from jax.experimental.pallas import tpu as pltpu

F32 = jnp.float32
BF16 = jnp.bfloat16

D_MODEL = 4096
DEPTH = 4
N_MIXERS = 3

A_CHUNK = 128
A_GROUPS = 16
A_GROUP_DIM = D_MODEL // A_GROUPS

B_PATTERNS = ((128, 1), (512, 4), (2048, 16))
B_HEADS = 16
B_HEAD_DIM = 128
B_ROT_DIM = B_HEAD_DIM // 4
B_QBLOCK = 128
B_WIDTH = B_HEADS * B_HEAD_DIM
ROPE_THETA = 500000.0

C_WINDOWS = (2, 4, 8, 16)
C_GROUP_DIM = D_MODEL // len(C_WINDOWS)
C_HALO = 16

D_FF = 11008
FF_BLOCK = 512
CONV_WIDTH = 3
CARRY_ROWS = 8

LN_EPS = 1e-5
LN_ROWS = 8
DEEPNORM_ALPHA = (2 * DEPTH) ** 0.25

LANES = 128
V7X_VMEM_LIMIT_BYTES = 56 * 1024 * 1024


def _params(*semantics, flags=None):
    return pltpu.CompilerParams(dimension_semantics=semantics, vmem_limit_bytes=V7X_VMEM_LIMIT_BYTES,
                                flags=flags)


def _layer_norm(z, g, b):
    mu = jnp.mean(z, axis=-1, keepdims=True)
    d = z - mu
    var = jnp.mean(d * d, axis=-1, keepdims=True)
    return d * lax.rsqrt(var + LN_EPS) * g + b


def _cast_slab_rows(total_rows, n_steps, smallest=16):
    rows = smallest
    while total_rows % rows or total_rows // rows > n_steps:
        rows *= 2
        assert rows <= total_rows
    return rows


def _side_cast_plumbing(jobs, n_steps, step_of):
    in_specs, out_specs, out_shapes = [], [], []
    for src, split in jobs:
        rows, width = src.shape
        slab = _cast_slab_rows(rows, n_steps)
        index = lambda *g, last=rows // slab - 1: (jnp.minimum(step_of(*g), last), 0)
        in_specs.append(pl.BlockSpec((slab, width), index))
        parts = 2 if split else 1
        out_specs += [pl.BlockSpec((slab, width // parts), index)] * parts
        out_shapes += [jax.ShapeDtypeStruct((rows, width // parts), BF16)] * parts
    return in_specs, out_specs, out_shapes


def _run_side_casts(src_refs, dst_refs):
    dst = list(dst_refs)
    for src in src_refs:
        width = dst[0].shape[1]
        for part in range(src.shape[1] // width):
            dst.pop(0)[...] = src[:, part * width:(part + 1) * width].astype(BF16)
    assert not dst


def _gelu(x):
    return 0.5 * x * (1.0 + lax.erf(x * (1.0 / math.sqrt(2.0))))


def _mm_kernel(*refs, epilogue, n_jobs):
    x_ref, w_ref = refs[:2]
    o_ref = refs[2 + n_jobs]
    _run_side_casts(refs[2:2 + n_jobs], refs[3 + n_jobs:])
    acc = jnp.dot(x_ref[...], w_ref[...], preferred_element_type=F32)
    o_ref[...] = epilogue(acc).astype(o_ref.dtype)


def _mm(x, w, *, epilogue, out_dtype, name, cast_jobs=(), tm=1024, tn=1024):
    m, k = x.shape
    n = w.shape[1]
    nj = n // tn
    c_in, c_out, c_shapes = _side_cast_plumbing(cast_jobs, (m // tm) * nj, lambda i, j: i * nj + j)
    res = pl.pallas_call(
        functools.partial(_mm_kernel, epilogue=epilogue, n_jobs=len(cast_jobs)),
        grid=(m // tm, nj),
        in_specs=[pl.BlockSpec((tm, k), lambda i, j: (i, 0)),
                  pl.BlockSpec((k, tn), lambda i, j: (0, j))] + c_in,
        out_specs=[pl.BlockSpec((tm, tn), lambda i, j: (i, j))] + c_out,
        out_shape=[jax.ShapeDtypeStruct((m, n), out_dtype)] + c_shapes,
        compiler_params=_params("arbitrary", "arbitrary"),
        name=name,
    )(x, w, *[src for src, _ in cast_jobs])
    return res[0], res[1:]


def _qkv_kernel(x_ref, w_ref, pos_ref, invf_ref, o_ref, cos_ref, sin_ref, *perm_scratch,
                tm, tn, dilation):
    j = pl.program_id(1)

    @pl.when(j == 0)
    def _():
        ang = pos_ref[...] * invf_ref[...]
        cos_ref[...] = jnp.cos(ang)
        sin_ref[...] = jnp.sin(ang)

    acc = jnp.dot(x_ref[...], w_ref[...], preferred_element_type=F32)
    rot = jnp.where((j * tn) // B_WIDTH < 2, 1.0, 0.0).astype(F32)
    cos1 = 1.0 + (cos_ref[...] - 1.0) * rot
    sin1 = sin_ref[...] * rot
    reps = tn // B_HEAD_DIM
    cos = jnp.concatenate([cos1] * reps, axis=1)
    sin = jnp.concatenate([sin1] * reps, axis=1)
    lane = lax.broadcasted_iota(jnp.int32, acc.shape, 1) & (B_HEAD_DIM - 1)
    half = B_ROT_DIM // 2
    partner = jnp.where(lane < half,
                        pltpu.roll(acc, tn - half, 1),
                        pltpu.roll(acc, half, 1))
    out = acc * cos + partner * sin

    chunk = B_QBLOCK * dilation
    if dilation == 1:
        for c in range(tm // chunk):
            o_ref[c, 0] = out[c * chunk:(c + 1) * chunk].astype(o_ref.dtype)
        return
    (buf,) = perm_scratch
    for t in range(tn // LANES):
        cols = slice(t * LANES, (t + 1) * LANES)
        buf[t] = out[:, cols]
        if chunk <= tm:
            for c in range(tm // chunk):
                for r in range(dilation):
                    rows = pl.ds(c * chunk + r, B_QBLOCK, stride=dilation)
                    o_ref[c, r, :, cols] = buf[t, rows, :].astype(o_ref.dtype)
        else:
            for r in range(dilation):
                rows = pl.ds(r, tm // dilation, stride=dilation)
                o_ref[r, :, cols] = buf[t, rows, :].astype(o_ref.dtype)


def _qkv_proj(x, w, group, posb, invf, dilation, *, tm=1024, tn=1024):
    m, k = x.shape
    n = 3 * B_WIDTH
    first = group * n // tn
    chunk = B_QBLOCK * dilation
    if chunk <= tm:
        out_spec = pl.BlockSpec((tm // chunk, dilation, B_QBLOCK, tn), lambda i, j: (i, 0, 0, j))
    else:
        tiles_per_chunk = chunk // tm
        out_spec = pl.BlockSpec((None, dilation, tm // dilation, tn),
                                lambda i, j: (i // tiles_per_chunk, 0, i % tiles_per_chunk, j))
    scratch = [pltpu.VMEM((tm, B_HEAD_DIM), F32), pltpu.VMEM((tm, B_HEAD_DIM), F32)]
    if dilation > 1:
        scratch.append(pltpu.VMEM((tn // LANES, tm, LANES), F32))
    return pl.pallas_call(
        functools.partial(_qkv_kernel, tm=tm, tn=tn, dilation=dilation),
        grid=(m // tm, n // tn),
        in_specs=[pl.BlockSpec((tm, k), lambda i, j: (i, 0)),
                  pl.BlockSpec((k, tn), lambda i, j: (0, first + j)),
                  pl.BlockSpec((tm, B_HEAD_DIM), lambda i, j: (i, 0)),
                  pl.BlockSpec((1, B_HEAD_DIM), lambda i, j: (0, 0))],
        out_specs=out_spec,
        out_shape=jax.ShapeDtypeStruct((m // chunk, dilation, B_QBLOCK, n), BF16),
        scratch_shapes=scratch,
        compiler_params=_params("arbitrary", "arbitrary"),
        name=f"qkv_proj_d{dilation}",
    )(x, w, posb, invf)


def _attn_kernel(q_ref, kp_ref, kc_ref, vp_ref, vc_ref, o_ref, l_ref, *, chunks_per_seq):
    c = pl.program_id(0)
    has_prev = (c % chunks_per_seq) > 0
    q_blk = B_QBLOCK
    qi = lax.broadcasted_iota(jnp.int32, (q_blk, 2 * q_blk), 0)
    kj = lax.broadcasted_iota(jnp.int32, (q_blk, 2 * q_blk), 1)
    mask = (kj >= qi) & (kj <= qi + q_blk) & ((kj >= q_blk) | has_prev)
    scale = B_HEAD_DIM ** -0.5
    head_lane = lax.broadcasted_iota(jnp.int32, (q_blk, LANES), 1)
    lse_tile = jnp.zeros((q_blk, LANES), F32)
    for h in range(B_HEADS):
        sl = slice(h * B_HEAD_DIM, (h + 1) * B_HEAD_DIM)
        q = q_ref[:, sl]
        k2 = jnp.concatenate([kp_ref[:, sl], kc_ref[:, sl]], axis=0)
        v2 = jnp.concatenate([vp_ref[:, sl], vc_ref[:, sl]], axis=0)
        s = lax.dot_general(q, k2, (((1,), (1,)), ((), ())), preferred_element_type=F32) * scale
        s = jnp.where(mask, s, -jnp.inf)
        mx = jnp.max(s, axis=-1, keepdims=True)
        p = jnp.exp(s - mx)
        den = jnp.sum(p, axis=-1, keepdims=True)
        o = jnp.dot(p.astype(BF16), v2, preferred_element_type=F32) / den
        o_ref[:, sl] = o.astype(o_ref.dtype)
        lse_tile = jnp.where(head_lane == h, mx + jnp.log(den), lse_tile)
    l_ref[...] = lse_tile


def _dilated_attention(qkv, seq, dilation):
    n_chunks = qkv.shape[0]
    chunks_per_seq = seq // (B_QBLOCK * dilation)

    def spec(which, prev):
        def index(c, r):
            return (jnp.maximum(c - 1, 0) if prev else c, r, 0, which)
        return pl.BlockSpec((None, None, B_QBLOCK, B_WIDTH), index)

    return pl.pallas_call(
        functools.partial(_attn_kernel, chunks_per_seq=chunks_per_seq),
        grid=(n_chunks, dilation),
        in_specs=[spec(0, False), spec(1, True), spec(1, False), spec(2, True), spec(2, False)],
        out_specs=[pl.BlockSpec((None, None, B_QBLOCK, B_WIDTH), lambda c, r: (c, r, 0, 0)),
                   pl.BlockSpec((None, None, B_QBLOCK, LANES), lambda c, r: (c, r, 0, 0))],
        out_shape=[jax.ShapeDtypeStruct((n_chunks, dilation, B_QBLOCK, B_WIDTH), BF16),
                   jax.ShapeDtypeStruct((n_chunks, dilation, B_QBLOCK, LANES), F32)],
        compiler_params=_params("parallel", "parallel"),
        name=f"dilated_attn_d{dilation}",
    )(qkv, qkv, qkv, qkv, qkv)


def _combine_kernel(*refs, tm, dilations):
    n = len(dilations)
    o_refs, l_refs, out_ref = refs[:n], refs[n:2 * n], refs[2 * n]
    o_bufs, l_bufs = refs[2 * n + 1:3 * n + 1], refs[3 * n + 1:4 * n + 1]

    def scatter(src_o, src_l, o_buf, l_buf, rows):
        l_buf[rows, :] = src_l
        for h in range(B_HEADS):
            o_buf[h, rows, :] = src_o[:, h * B_HEAD_DIM:(h + 1) * B_HEAD_DIM].astype(F32)

    for o_ref, l_ref, o_buf, l_buf, d in zip(o_refs, l_refs, o_bufs, l_bufs, dilations):
        chunk = B_QBLOCK * d
        if chunk <= tm:
            for c in range(tm // chunk):
                for r in range(d):
                    rows = pl.ds(c * chunk + r, B_QBLOCK, stride=d) if d > 1 else pl.ds(c * chunk, B_QBLOCK)
                    scatter(o_ref[c, r], l_ref[c, r], o_buf, l_buf, rows)
        else:
            for r in range(d):
                scatter(o_ref[r], l_ref[r], o_buf, l_buf, pl.ds(r, tm // d, stride=d))

    lses = [l_buf[...] for l_buf in l_bufs]
    mx = functools.reduce(jnp.maximum, lses)
    es = [jnp.exp(l - mx) for l in lses]
    tot = functools.reduce(lambda a, b: a + b, es)
    ws = [e / tot for e in es]
    for h in range(B_HEADS):
        acc = None
        for w, o_buf in zip(ws, o_bufs):
            term = jnp.broadcast_to(w[:, h:h + 1], (tm, B_HEAD_DIM)) * o_buf[h]
            acc = term if acc is None else acc + term
        out_ref[:, h * B_HEAD_DIM:(h + 1) * B_HEAD_DIM] = acc.astype(out_ref.dtype)


def _combine_groups(outs, lses, dilations, m, *, tm=512):
    def spec(d, width):
        chunk = B_QBLOCK * d
        if chunk <= tm:
            return pl.BlockSpec((tm // chunk, d, B_QBLOCK, width), lambda i: (i, 0, 0, 0))
        tiles_per_chunk = chunk // tm
        return pl.BlockSpec((None, d, tm // d, width),
                            lambda i: (i // tiles_per_chunk, 0, i % tiles_per_chunk, 0))

    n = len(dilations)
    return pl.pallas_call(
        functools.partial(_combine_kernel, tm=tm, dilations=dilations),
        grid=(m // tm,),
        in_specs=[spec(d, B_WIDTH) for d in dilations] + [spec(d, LANES) for d in dilations],
        out_specs=pl.BlockSpec((tm, B_WIDTH), lambda i: (i, 0)),
        out_shape=jax.ShapeDtypeStruct((m, B_WIDTH), BF16),
        scratch_shapes=[pltpu.VMEM((B_HEADS, tm, B_HEAD_DIM), F32)] * n + [pltpu.VMEM((tm, LANES), F32)] * n,
        compiler_params=_params("parallel"),
        name="attn_combine",
    )(*outs, *lses)


def _gate_kernel(u_ref, v_ref, g_ref, b_ref, ws_ref, bs_ref, o_ref, *, rows):
    vn = _layer_norm(v_ref[...].astype(F32), g_ref[...], b_ref[...]).astype(BF16)
    ti = lax.broadcasted_iota(jnp.int32, (A_CHUNK, A_CHUNK), 0)
    si = lax.broadcasted_iota(jnp.int32, (A_CHUNK, A_CHUNK), 1)
    causal = si <= ti
    for g in range(A_GROUPS):
        cols = slice(g * A_GROUP_DIM, (g + 1) * A_GROUP_DIM)
        w = jnp.where(causal, ws_ref[g], 0.0).astype(BF16)
        bias = bs_ref[:, cols]
        for c in range(rows // A_CHUNK):
            rws = slice(c * A_CHUNK, (c + 1) * A_CHUNK)
            mixed = jnp.dot(w, vn[rws, cols], preferred_element_type=F32) + bias
            o_ref[rws, cols] = (u_ref[rws, cols].astype(F32) * mixed).astype(o_ref.dtype)


def _spatial_gate(h, ln_g, ln_b, w_s, bs_exp, *, rows=256):
    m = h.shape[0]
    width = D_MODEL
    return pl.pallas_call(
        functools.partial(_gate_kernel, rows=rows),
        grid=(m // rows,),
        in_specs=[pl.BlockSpec((rows, width), lambda i: (i, 0)),
                  pl.BlockSpec((rows, width), lambda i: (i, 1)),
                  pl.BlockSpec((1, width), lambda i: (0, 0)),
                  pl.BlockSpec((1, width), lambda i: (0, 0)),
                  pl.BlockSpec((A_GROUPS, A_CHUNK, A_CHUNK), lambda i: (0, 0, 0)),
                  pl.BlockSpec((A_CHUNK, width), lambda i: (0, 0))],
        out_specs=pl.BlockSpec((rows, width), lambda i: (i, 0)),
        out_shape=jax.ShapeDtypeStruct((m, width), BF16),
        compiler_params=_params("parallel"),
        name="spatial_gate",
    )(h, h, ln_g, ln_b, w_s, bs_exp)


def _pool_kernel(h_ref, halo_ref, wg_ref, scale_ref, o_ref, *, rows, seq):
    i = pl.program_id(0)
    t0 = (i * rows) % seq
    pos = t0 + lax.broadcasted_iota(jnp.int32, (rows, 1), 0)
    keep_halo = t0 > 0
    for g, window in enumerate(C_WINDOWS):
        cols = slice(g * C_GROUP_DIM, (g + 1) * C_GROUP_DIM)
        hg = h_ref[:, cols]
        halo = jnp.where(keep_halo, halo_ref[:, cols], 0.0)
        acc = jnp.concatenate([halo, hg], axis=0)
        span = 1
        while span < window:
            acc = acc + pltpu.roll(acc, span, 0)
            span *= 2
        den = jnp.minimum(pos + 1, window).astype(F32)
        pooled = acc[C_HALO:, :] / den - hg
        y = jnp.dot(pooled.astype(BF16), wg_ref[g], preferred_element_type=F32) * scale_ref[:, cols]
        o_ref[:, cols] = y.astype(o_ref.dtype)


def _pool_mix(h, w_group, scale, seq, *, rows=256):
    m, width = h.shape
    halo_per_block = rows // C_HALO
    return pl.pallas_call(
        functools.partial(_pool_kernel, rows=rows, seq=seq),
        grid=(m // rows,),
        in_specs=[pl.BlockSpec((rows, width), lambda i: (i, 0)),
                  pl.BlockSpec((C_HALO, width), lambda i: (jnp.maximum(i * halo_per_block - 1, 0), 0)),
                  pl.BlockSpec((len(C_WINDOWS), C_GROUP_DIM, C_GROUP_DIM), lambda i: (0, 0, 0)),
                  pl.BlockSpec((1, width), lambda i: (0, 0))],
        out_specs=pl.BlockSpec((rows, width), lambda i: (i, 0)),
        out_shape=jax.ShapeDtypeStruct((m, width), BF16),
        compiler_params=_params("parallel"),
        name="pool_mix",
    )(h, h, w_group, scale)


def _ffn_up_kernel(*refs, tm, tiles_per_seq, n_jobs):
    x_ref, wg_ref, wu_ref, cwg_ref, cwu_ref, cbg_ref, cbu_ref = refs[:7]
    o_ref = refs[7 + n_jobs]
    hbuf, carry = refs[-2:]
    i = pl.program_id(0)
    j = pl.program_id(1)
    seq_start = (i % tiles_per_seq) == 0

    @pl.when(seq_start)
    def _():
        hbuf[:, 0:CARRY_ROWS, :] = jnp.zeros((2, CARRY_ROWS, FF_BLOCK), F32)

    @pl.when(jnp.logical_not(seq_start))
    def _():
        hbuf[:, 0:CARRY_ROWS, :] = carry[j]

    _run_side_casts(refs[7:7 + n_jobs], refs[8 + n_jobs:-2])

    x = x_ref[...]
    halves = []
    for s, (w_ref, cw_ref, cb_ref) in enumerate(((wg_ref, cwg_ref, cbg_ref), (wu_ref, cwu_ref, cbu_ref))):
        h = jnp.dot(x, w_ref[...], preferred_element_type=F32)
        hbuf[s, CARRY_ROWS:, :] = h
        carry[j, s] = h[tm - CARRY_ROWS:, :]
        h1 = hbuf[s, CARRY_ROWS - 1:CARRY_ROWS - 1 + tm, :]
        h2 = hbuf[s, CARRY_ROWS - 2:CARRY_ROWS - 2 + tm, :]
        halves.append(cw_ref[2:3, :] * h + cw_ref[1:2, :] * h1 + cw_ref[0:1, :] * h2 + cb_ref[...])
    gate, up = halves
    o_ref[...] = (gate / (1.0 + jnp.exp(-gate)) * up).astype(o_ref.dtype)


def _ffn_up(x, w_gate, w_up, cw_gate, cw_up, cb_gate, cb_up, seq, *, cast_jobs=(), tm=1024):
    m, k = x.shape
    d_ff = w_gate.shape[1]
    nj = pl.cdiv(d_ff, FF_BLOCK)
    col = lambda i, j: (0, j)
    c_in, c_out, c_shapes = _side_cast_plumbing(cast_jobs, (m // tm) * nj, lambda i, j: i * nj + j)
    res = pl.pallas_call(
        functools.partial(_ffn_up_kernel, tm=tm, tiles_per_seq=seq // tm, n_jobs=len(cast_jobs)),
        grid=(m // tm, nj),
        in_specs=[pl.BlockSpec((tm, k), lambda i, j: (i, 0)),
                  pl.BlockSpec((k, FF_BLOCK), col), pl.BlockSpec((k, FF_BLOCK), col),
                  pl.BlockSpec((CONV_WIDTH, FF_BLOCK), col), pl.BlockSpec((CONV_WIDTH, FF_BLOCK), col),
                  pl.BlockSpec((1, FF_BLOCK), col), pl.BlockSpec((1, FF_BLOCK), col)] + c_in,
        out_specs=[pl.BlockSpec((tm, FF_BLOCK), lambda i, j: (i, j))] + c_out,
        out_shape=[jax.ShapeDtypeStruct((m, d_ff), BF16)] + c_shapes,
        scratch_shapes=[pltpu.VMEM((2, tm + CARRY_ROWS, FF_BLOCK), F32),
                        pltpu.VMEM((nj, 2, CARRY_ROWS, FF_BLOCK), F32)],
        compiler_params=_params("arbitrary", "arbitrary"),
        name="ffn_up_conv_gate",
    )(x, w_gate, w_up, cw_gate, cw_up, cb_gate, cb_up, *[src for src, _ in cast_jobs])
    return res[0], res[1:]


def _proj_ln_kernel(*refs, nk, k_tail, n_chunks, n_jobs):
    a_ref, w_ref, x_ref, g_ref, b_ref = refs[:5]
    of_ref, ob_ref = refs[5 + n_jobs:7 + n_jobs]
    mu_ref, rstd_ref = refs[-2:]
    _run_side_casts(refs[5:5 + n_jobs], refs[7 + n_jobs:-2])
    k = pl.program_id(1)
    tm, n = of_ref.shape

    @pl.when(k == 0)
    def _():
        of_ref[...] = jnp.dot(a_ref[...], w_ref[...], preferred_element_type=F32)

    if k_tail == a_ref.shape[1]:
        @pl.when(k > 0)
        def _():
            of_ref[...] += jnp.dot(a_ref[...], w_ref[...], preferred_element_type=F32)
    else:
        @pl.when((k > 0) & (k < nk - 1))
        def _():
            of_ref[...] += jnp.dot(a_ref[...], w_ref[...], preferred_element_type=F32)

        @pl.when(k == nk - 1)
        def _():
            of_ref[...] += jnp.dot(a_ref[:, :k_tail], w_ref[:k_tail, :], preferred_element_type=F32)

    cw = n // n_chunks
    for c in range(n_chunks):
        @pl.when(k == c)
        def _(c=c):
            of_ref[:, c * cw:(c + 1) * cw] += DEEPNORM_ALPHA * x_ref[...]

    @pl.when(k == nk - 1)
    def _():
        steps = tm // LN_ROWS

        def rows_at(r):
            return pl.ds(pl.multiple_of(r * LN_ROWS, LN_ROWS), LN_ROWS)

        def mean_pass(r, carry):
            z = of_ref[rows_at(r), :]
            mu_ref[rows_at(r), :] = jnp.broadcast_to(jnp.mean(z, axis=-1, keepdims=True), (LN_ROWS, LANES))
            return carry

        def across(stat_ref, r):
            return jnp.concatenate([stat_ref[rows_at(r), :]] * (n // LANES), axis=1)

        def var_pass(r, carry):
            d = of_ref[rows_at(r), :] - across(mu_ref, r)
            var = jnp.mean(d * d, axis=-1, keepdims=True)
            rstd_ref[rows_at(r), :] = jnp.broadcast_to(lax.rsqrt(var + LN_EPS), (LN_ROWS, LANES))
            return carry

        def norm_pass(r, carry):
            d = of_ref[rows_at(r), :] - across(mu_ref, r)
            y = d * across(rstd_ref, r) * g_ref[...] + b_ref[...]
            of_ref[rows_at(r), :] = y
            ob_ref[rows_at(r), :] = y.astype(ob_ref.dtype)
            return carry

        lax.fori_loop(0, steps, mean_pass, 0, unroll=16)
        lax.fori_loop(0, steps, var_pass, 0, unroll=16)
        lax.fori_loop(0, steps, norm_pass, 0, unroll=4)


def _proj_ln(a, w, x, g, b, *, name, cast_jobs=(), tm=512, tk=1024):
    m, kdim = a.shape
    n = w.shape[1]
    nk = pl.cdiv(kdim, tk)
    k_tail = kdim - (nk - 1) * tk
    n_chunks = 1
    while n_chunks * 2 <= min(nk, 8):
        n_chunks *= 2
    cw = n // n_chunks
    c_in, c_out, c_shapes = _side_cast_plumbing(cast_jobs, (m // tm) * nk, lambda i, k: i * nk + k)
    res = pl.pallas_call(
        functools.partial(_proj_ln_kernel, nk=nk, k_tail=k_tail, n_chunks=n_chunks, n_jobs=len(cast_jobs)),
        grid=(m // tm, nk),
        in_specs=[pl.BlockSpec((tm, tk), lambda i, k: (i, k)),
                  pl.BlockSpec((tk, n), lambda i, k: (k, 0)),
                  pl.BlockSpec((tm, cw), lambda i, k: (i, jnp.minimum(k, n_chunks - 1))),
                  pl.BlockSpec((LN_ROWS, n), lambda i, k: (0, 0)),
                  pl.BlockSpec((LN_ROWS, n), lambda i, k: (0, 0))] + c_in,
        out_specs=[pl.BlockSpec((tm, n), lambda i, k: (i, 0)),
                   pl.BlockSpec((tm, n), lambda i, k: (i, 0))] + c_out,
        out_shape=[jax.ShapeDtypeStruct((m, n), F32), jax.ShapeDtypeStruct((m, n), BF16)] + c_shapes,
        scratch_shapes=[pltpu.VMEM((tm, LANES), F32), pltpu.VMEM((tm, LANES), F32)],
        compiler_params=_params("arbitrary", "arbitrary"),
        name=name,
    )(a, w, x, jnp.broadcast_to(g.reshape(1, n), (LN_ROWS, n)), jnp.broadcast_to(b.reshape(1, n), (LN_ROWS, n)),
      *[src for src, _ in cast_jobs])
    return res[0], res[1], res[2:]


def _rotary_inv_freq():
    half = B_ROT_DIM // 2
    inv_freq = jnp.float32(ROPE_THETA) ** (-jnp.arange(half, dtype=F32) * 2.0 / B_ROT_DIM)
    zeros = jnp.zeros((B_HEAD_DIM - B_ROT_DIM,), F32)
    return jnp.concatenate([-inv_freq, inv_freq, zeros]).reshape(1, B_HEAD_DIM)


def kernel(x, positions, a_w_in, a_ln_g, a_ln_b, a_w_s, a_b_s, a_w_out, b_w_in, b_w_out, c_w_in, c_w_group, c_scale, c_w_out, f_w_up, f_conv_w, f_conv_b, f_w_down, ln_mix_g, ln_mix_b, ln_ffn_g, ln_ffn_b):
    batch, seq, d = x.shape
    m = batch * seq
    xf = x.reshape(m, d)
    xb = xf.astype(BF16)
    posb = jnp.broadcast_to(positions.reshape(m, 1).astype(F32), (m, B_HEAD_DIM))
    invf = _rotary_inv_freq()

    def group_w(t):
        return t.reshape(-1, t.shape[-1])

    a_in = [a_w_in[0].astype(BF16), None]
    a_out = [None, None]
    ffn_up_w = None
    for layer in range(DEPTH):
        kind, idx = layer % N_MIXERS, layer // N_MIXERS
        if kind == 0:
            jobs = [(f_w_up[0], True), (a_w_out[0], False)] if layer == 0 else []
            h, casts = _mm(xb, a_in[idx], epilogue=_gelu, out_dtype=BF16, name="gmlp_in_gelu", cast_jobs=jobs)
            if layer == 0:
                ffn_up_w, a_out[0] = casts[:2], casts[2]
            bs_exp = jnp.repeat(a_b_s[idx].T, A_GROUP_DIM, axis=1)
            y = _spatial_gate(h, a_ln_g[idx].reshape(1, d), a_ln_b[idx].reshape(1, d), a_w_s[idx], bs_exp)
            w_out = a_out[idx]
        elif kind == 1:
            outs, lses, dilations = [], [], []
            for group, (window, dilation) in enumerate(B_PATTERNS):
                assert window // dilation == B_QBLOCK
                qkv = _qkv_proj(xb, b_in, group, posb, invf, dilation)
                o, lse = _dilated_attention(qkv, seq, dilation)
                outs.append(o)
                lses.append(lse)
                dilations.append(dilation)
            y = _combine_groups(outs, lses, tuple(dilations), m)
            w_out = b_out
        else:
            h, _ = _mm(xb, c_in, epilogue=lambda t: t, out_dtype=F32, name="pool_in")
            y = _pool_mix(h, c_group.reshape(c_w_group[idx].shape), c_scale[idx].reshape(1, d), seq)
            w_out = c_out
        xf, xb, _ = _proj_ln(y, w_out, xf, ln_mix_g[layer], ln_mix_b[layer], name="mixer_out_ln")

        jobs = [(f_w_down[layer], False)]
        nxt = layer + 1
        if nxt < DEPTH and nxt % N_MIXERS == 1:
            jobs += [(b_w_in[nxt // N_MIXERS], False), (b_w_out[nxt // N_MIXERS], False)]
        elif nxt < DEPTH and nxt % N_MIXERS == 2:
            jobs += [(c_w_in[nxt // N_MIXERS], False), (c_w_out[nxt // N_MIXERS], False),
                     (group_w(c_w_group[nxt // N_MIXERS]), False)]
        elif nxt < DEPTH:
            jobs += [(a_w_in[nxt // N_MIXERS], False), (a_w_out[nxt // N_MIXERS], False)]
        cw, cb = f_conv_w[layer], f_conv_b[layer].reshape(1, -1)
        g, casts = _ffn_up(xb, ffn_up_w[0], ffn_up_w[1], cw[:, :D_FF], cw[:, D_FF:], cb[:, :D_FF], cb[:, D_FF:],
                           seq, cast_jobs=jobs)
        w_down = casts[0]
        if nxt < DEPTH and nxt % N_MIXERS == 1:
            b_in, b_out = casts[1:3]
        elif nxt < DEPTH and nxt % N_MIXERS == 2:
            c_in, c_out, c_group = casts[1:4]
        elif nxt < DEPTH:
            a_in[nxt // N_MIXERS], a_out[nxt // N_MIXERS] = casts[1:3]
        jobs = [(f_w_up[nxt], True)] if nxt < DEPTH else []
        xf, xb, casts = _proj_ln(g, w_down, xf, ln_ffn_g[layer], ln_ffn_b[layer], name="ffn_down_ln",
                                 cast_jobs=jobs)
        if nxt < DEPTH:
            ffn_up_w = casts

    return xf.reshape(batch, seq, d)
```

```python
import functools
import math

import jax
import jax.numpy as jnp
from jax import lax
from jax.experimental import pallas as pl
from jax.experimental.pallas import tpu as pltpu

F32 = jnp.float32
BF16 = jnp.bfloat16

D_MODEL = 4096
DEPTH = 4
N_MIXERS = 3

A_CHUNK = 128
A_GROUPS = 16
A_GROUP_DIM = D_MODEL // A_GROUPS

B_PATTERNS = ((128, 1), (512, 4), (2048, 16))
B_HEADS = 16
B_HEAD_DIM = 128
B_ROT_DIM = B_HEAD_DIM // 4
B_QBLOCK = 128
B_WIDTH = B_HEADS * B_HEAD_DIM
ROPE_THETA = 500000.0

C_WINDOWS = (2, 4, 8, 16)
C_GROUP_DIM = D_MODEL // len(C_WINDOWS)
C_HALO = 16

D_FF = 11008
FF_BLOCK = 512
CONV_WIDTH = 3
CARRY_ROWS = 8

LN_EPS = 1e-5
LN_ROWS = 8
DEEPNORM_ALPHA = (2 * DEPTH) ** 0.25

LANES = 128
V7X_VMEM_LIMIT_BYTES = 56 * 1024 * 1024


def _params(*semantics, flags=None):
    return pltpu.CompilerParams(dimension_semantics=semantics, vmem_limit_bytes=V7X_VMEM_LIMIT_BYTES,
                                flags=flags)


def _layer_norm(z, g, b):
    mu = jnp.mean(z, axis=-1, keepdims=True)
    d = z - mu
    var = jnp.mean(d * d, axis=-1, keepdims=True)
    return d * lax.rsqrt(var + LN_EPS) * g + b


def _cast_slab_rows(total_rows, n_steps, smallest=16):
    rows = smallest
    while total_rows % rows or total_rows // rows > n_steps:
        rows *= 2
        assert rows <= total_rows
    return rows


def _side_cast_plumbing(jobs, n_steps, step_of):
    in_specs, out_specs, out_shapes = [], [], []
    for src, layer, split in jobs:
        _, rows, width = src.shape
        slab = _cast_slab_rows(rows, n_steps)
        slab_of = lambda *g, last=rows // slab - 1: jnp.minimum(step_of(*g), last)
        in_specs.append(pl.BlockSpec((None, slab, width), lambda *g, f=slab_of, l=layer: (l, f(*g), 0)))
        parts = 2 if split else 1
        out_specs += [pl.BlockSpec((slab, width // parts), lambda *g, f=slab_of: (f(*g), 0))] * parts
        out_shapes += [jax.ShapeDtypeStruct((rows, width // parts), BF16)] * parts
    return in_specs, out_specs, out_shapes


def _run_side_casts(src_refs, dst_refs):
    dst = list(dst_refs)
    for src in src_refs:
        width = dst[0].shape[1]
        for part in range(src.shape[1] // width):
            dst.pop(0)[...] = src[:, part * width:(part + 1) * width].astype(BF16)
    assert not dst


def _gelu(x):
    return 0.5 * x * (1.0 + lax.erf(x * (1.0 / math.sqrt(2.0))))


def _mm_kernel(*refs, epilogue, n_jobs):
    x_ref, w_ref = refs[:2]
    o_ref = refs[2 + n_jobs]
    _run_side_casts(refs[2:2 + n_jobs], refs[3 + n_jobs:])
    acc = jnp.dot(x_ref[...], w_ref[...], preferred_element_type=F32)
    o_ref[...] = epilogue(acc).astype(o_ref.dtype)


def _mm(x, w, *, epilogue, out_dtype, name, cast_jobs=(), tm=1024, tn=1024):
    m, k = x.shape
    n = w.shape[1]
    nj = n // tn
    c_in, c_out, c_shapes = _side_cast_plumbing(cast_jobs, (m // tm) * nj, lambda i, j: i * nj + j)
    res = pl.pallas_call(
        functools.partial(_mm_kernel, epilogue=epilogue, n_jobs=len(cast_jobs)),
        grid=(m // tm, nj),
        in_specs=[pl.BlockSpec((tm, k), lambda i, j: (i, 0)),
                  pl.BlockSpec((k, tn), lambda i, j: (0, j))] + c_in,
        out_specs=[pl.BlockSpec((tm, tn), lambda i, j: (i, j))] + c_out,
        out_shape=[jax.ShapeDtypeStruct((m, n), out_dtype)] + c_shapes,
        compiler_params=_params("arbitrary", "arbitrary"),
        name=name,
    )(x, w, *[job[0] for job in cast_jobs])
    return res[0], res[1:]


def _qkv_kernel(x_ref, w_ref, pos_ref, invf_ref, o_ref, cos_ref, sin_ref, *perm_scratch,
                tm, tn, dilation):
    j = pl.program_id(1)

    @pl.when(j == 0)
    def _():
        ang = pos_ref[...] * invf_ref[...]
        cos_ref[...] = jnp.cos(ang)
        sin_ref[...] = jnp.sin(ang)

    acc = jnp.dot(x_ref[...], w_ref[...], preferred_element_type=F32)
    rot = jnp.where((j * tn) // B_WIDTH < 2, 1.0, 0.0).astype(F32)
    cos1 = 1.0 + (cos_ref[...] - 1.0) * rot
    sin1 = sin_ref[...] * rot
    reps = tn // B_HEAD_DIM
    cos = jnp.concatenate([cos1] * reps, axis=1)
    sin = jnp.concatenate([sin1] * reps, axis=1)
    lane = lax.broadcasted_iota(jnp.int32, acc.shape, 1) & (B_HEAD_DIM - 1)
    half = B_ROT_DIM // 2
    partner = jnp.where(lane < half,
                        pltpu.roll(acc, tn - half, 1),
                        pltpu.roll(acc, half, 1))
    out = acc * cos + partner * sin

    chunk = B_QBLOCK * dilation
    if dilation == 1:
        for c in range(tm // chunk):
            o_ref[c, 0] = out[c * chunk:(c + 1) * chunk].astype(o_ref.dtype)
        return
    (buf,) = perm_scratch
    for t in range(tn // LANES):
        cols = slice(t * LANES, (t + 1) * LANES)
        buf[t] = out[:, cols]
        if chunk <= tm:
            for c in range(tm // chunk):
                for r in range(dilation):
                    rows = pl.ds(c * chunk + r, B_QBLOCK, stride=dilation)
                    o_ref[c, r, :, cols] = buf[t, rows, :].astype(o_ref.dtype)
        else:
            for r in range(dilation):
                rows = pl.ds(r, tm // dilation, stride=dilation)
                o_ref[r, :, cols] = buf[t, rows, :].astype(o_ref.dtype)


def _qkv_proj(x, w, group, posb, invf, dilation, *, tm=1024, tn=1024):
    m, k = x.shape
    n = 3 * B_WIDTH
    first = group * n // tn
    chunk = B_QBLOCK * dilation
    if chunk <= tm:
        out_spec = pl.BlockSpec((tm // chunk, dilation, B_QBLOCK, tn), lambda i, j: (i, 0, 0, j))
    else:
        tiles_per_chunk = chunk // tm
        out_spec = pl.BlockSpec((None, dilation, tm // dilation, tn),
                                lambda i, j: (i // tiles_per_chunk, 0, i % tiles_per_chunk, j))
    scratch = [pltpu.VMEM((tm, B_HEAD_DIM), F32), pltpu.VMEM((tm, B_HEAD_DIM), F32)]
    if dilation > 1:
        scratch.append(pltpu.VMEM((tn // LANES, tm, LANES), F32))
    return pl.pallas_call(
        functools.partial(_qkv_kernel, tm=tm, tn=tn, dilation=dilation),
        grid=(m // tm, n // tn),
        in_specs=[pl.BlockSpec((tm, k), lambda i, j: (i, 0)),
                  pl.BlockSpec((k, tn), lambda i, j: (0, first + j)),
                  pl.BlockSpec((tm, B_HEAD_DIM), lambda i, j: (i, 0)),
                  pl.BlockSpec((1, B_HEAD_DIM), lambda i, j: (0, 0))],
        out_specs=out_spec,
        out_shape=jax.ShapeDtypeStruct((m // chunk, dilation, B_QBLOCK, n), BF16),
        scratch_shapes=scratch,
        compiler_params=_params("arbitrary", "arbitrary"),
        name=f"qkv_proj_d{dilation}",
    )(x, w, posb, invf)


def _attn_kernel(q_ref, kp_ref, kc_ref, vp_ref, vc_ref, o_ref, l_ref, *, chunks_per_seq):
    c = pl.program_id(0)
    has_prev = (c % chunks_per_seq) > 0
    q_blk = B_QBLOCK
    qi = lax.broadcasted_iota(jnp.int32, (q_blk, 2 * q_blk), 0)
    kj = lax.broadcasted_iota(jnp.int32, (q_blk, 2 * q_blk), 1)
    mask = (kj >= qi) & (kj <= qi + q_blk) & ((kj >= q_blk) | has_prev)
    scale = B_HEAD_DIM ** -0.5
    head_lane = lax.broadcasted_iota(jnp.int32, (q_blk, LANES), 1)
    lse_tile = jnp.zeros((q_blk, LANES), F32)
    for h in range(B_HEADS):
        sl = slice(h * B_HEAD_DIM, (h + 1) * B_HEAD_DIM)
        q = q_ref[:, sl]
        k2 = jnp.concatenate([kp_ref[:, sl], kc_ref[:, sl]], axis=0)
        v2 = jnp.concatenate([vp_ref[:, sl], vc_ref[:, sl]], axis=0)
        s = lax.dot_general(q, k2, (((1,), (1,)), ((), ())), preferred_element_type=F32) * scale
        s = jnp.where(mask, s, -jnp.inf)
        mx = jnp.max(s, axis=-1, keepdims=True)
        p = jnp.exp(s - mx)
        den = jnp.sum(p, axis=-1, keepdims=True)
        o = jnp.dot(p.astype(BF16), v2, preferred_element_type=F32) / den
        o_ref[:, sl] = o.astype(o_ref.dtype)
        lse_tile = jnp.where(head_lane == h, mx + jnp.log(den), lse_tile)
    l_ref[...] = lse_tile


def _dilated_attention(qkv, seq, dilation):
    n_chunks = qkv.shape[0]
    chunks_per_seq = seq // (B_QBLOCK * dilation)

    def spec(which, prev):
        def index(c, r):
            return (jnp.maximum(c - 1, 0) if prev else c, r, 0, which)
        return pl.BlockSpec((None, None, B_QBLOCK, B_WIDTH), index)

    return pl.pallas_call(
        functools.partial(_attn_kernel, chunks_per_seq=chunks_per_seq),
        grid=(n_chunks, dilation),
        in_specs=[spec(0, False), spec(1, True), spec(1, False), spec(2, True), spec(2, False)],
        out_specs=[pl.BlockSpec((None, None, B_QBLOCK, B_WIDTH), lambda c, r: (c, r, 0, 0)),
                   pl.BlockSpec((None, None, B_QBLOCK, LANES), lambda c, r: (c, r, 0, 0))],
        out_shape=[jax.ShapeDtypeStruct((n_chunks, dilation, B_QBLOCK, B_WIDTH), BF16),
                   jax.ShapeDtypeStruct((n_chunks, dilation, B_QBLOCK, LANES), F32)],
        compiler_params=_params("parallel", "parallel"),
        name=f"dilated_attn_d{dilation}",
    )(qkv, qkv, qkv, qkv, qkv)


def _combine_kernel(*refs, tm, dilations):
    n = len(dilations)
    o_refs, l_refs, out_ref = refs[:n], refs[n:2 * n], refs[2 * n]
    o_bufs, l_bufs = refs[2 * n + 1:3 * n + 1], refs[3 * n + 1:4 * n + 1]

    def scatter(src_o, src_l, o_buf, l_buf, rows):
        l_buf[rows, :] = src_l
        for h in range(B_HEADS):
            o_buf[h, rows, :] = src_o[:, h * B_HEAD_DIM:(h + 1) * B_HEAD_DIM].astype(F32)

    for o_ref, l_ref, o_buf, l_buf, d in zip(o_refs, l_refs, o_bufs, l_bufs, dilations):
        chunk = B_QBLOCK * d
        if chunk <= tm:
            for c in range(tm // chunk):
                for r in range(d):
                    rows = pl.ds(c * chunk + r, B_QBLOCK, stride=d) if d > 1 else pl.ds(c * chunk, B_QBLOCK)
                    scatter(o_ref[c, r], l_ref[c, r], o_buf, l_buf, rows)
        else:
            for r in range(d):
                scatter(o_ref[r], l_ref[r], o_buf, l_buf, pl.ds(r, tm // d, stride=d))

    lses = [l_buf[...] for l_buf in l_bufs]
    mx = functools.reduce(jnp.maximum, lses)
    es = [jnp.exp(l - mx) for l in lses]
    tot = functools.reduce(lambda a, b: a + b, es)
    ws = [e / tot for e in es]
    for h in range(B_HEADS):
        acc = None
        for w, o_buf in zip(ws, o_bufs):
            term = jnp.broadcast_to(w[:, h:h + 1], (tm, B_HEAD_DIM)) * o_buf[h]
            acc = term if acc is None else acc + term
        out_ref[:, h * B_HEAD_DIM:(h + 1) * B_HEAD_DIM] = acc.astype(out_ref.dtype)


def _combine_groups(outs, lses, dilations, m, *, tm=512):
    def spec(d, width):
        chunk = B_QBLOCK * d
        if chunk <= tm:
            return pl.BlockSpec((tm // chunk, d, B_QBLOCK, width), lambda i: (i, 0, 0, 0))
        tiles_per_chunk = chunk // tm
        return pl.BlockSpec((None, d, tm // d, width),
                            lambda i: (i // tiles_per_chunk, 0, i % tiles_per_chunk, 0))

    n = len(dilations)
    return pl.pallas_call(
        functools.partial(_combine_kernel, tm=tm, dilations=dilations),
        grid=(m // tm,),
        in_specs=[spec(d, B_WIDTH) for d in dilations] + [spec(d, LANES) for d in dilations],
        out_specs=pl.BlockSpec((tm, B_WIDTH), lambda i: (i, 0)),
        out_shape=jax.ShapeDtypeStruct((m, B_WIDTH), BF16),
        scratch_shapes=[pltpu.VMEM((B_HEADS, tm, B_HEAD_DIM), F32)] * n + [pltpu.VMEM((tm, LANES), F32)] * n,
        compiler_params=_params("parallel"),
        name="attn_combine",
    )(*outs, *lses)


def _gate_kernel(u_ref, v_ref, g_ref, b_ref, ws_ref, bs_ref, o_ref, *, rows):
    vn = _layer_norm(v_ref[...].astype(F32), g_ref[...], b_ref[...]).astype(BF16)
    ti = lax.broadcasted_iota(jnp.int32, (A_CHUNK, A_CHUNK), 0)
    si = lax.broadcasted_iota(jnp.int32, (A_CHUNK, A_CHUNK), 1)
    causal = si <= ti
    for g in range(A_GROUPS):
        cols = slice(g * A_GROUP_DIM, (g + 1) * A_GROUP_DIM)
        w = jnp.where(causal, ws_ref[g], 0.0).astype(BF16)
        bias = bs_ref[:, cols]
        for c in range(rows // A_CHUNK):
            rws = slice(c * A_CHUNK, (c + 1) * A_CHUNK)
            mixed = jnp.dot(w, vn[rws, cols], preferred_element_type=F32) + bias
            o_ref[rws, cols] = (u_ref[rws, cols].astype(F32) * mixed).astype(o_ref.dtype)


def _spatial_gate(h, ln_g, ln_b, w_s, bs_exp, *, rows=256):
    m = h.shape[0]
    width = D_MODEL
    return pl.pallas_call(
        functools.partial(_gate_kernel, rows=rows),
        grid=(m // rows,),
        in_specs=[pl.BlockSpec((rows, width), lambda i: (i, 0)),
                  pl.BlockSpec((rows, width), lambda i: (i, 1)),
                  pl.BlockSpec((1, width), lambda i: (0, 0)),
                  pl.BlockSpec((1, width), lambda i: (0, 0)),
                  pl.BlockSpec((A_GROUPS, A_CHUNK, A_CHUNK), lambda i: (0, 0, 0)),
                  pl.BlockSpec((A_CHUNK, width), lambda i: (0, 0))],
        out_specs=pl.BlockSpec((rows, width), lambda i: (i, 0)),
        out_shape=jax.ShapeDtypeStruct((m, width), BF16),
        compiler_params=_params("parallel"),
        name="spatial_gate",
    )(h, h, ln_g, ln_b, w_s, bs_exp)


def _pool_kernel(h_ref, halo_ref, wg_ref, scale_ref, o_ref, *, rows, seq):
    i = pl.program_id(0)
    t0 = (i * rows) % seq
    pos = t0 + lax.broadcasted_iota(jnp.int32, (rows, 1), 0)
    keep_halo = t0 > 0
    for g, window in enumerate(C_WINDOWS):
        cols = slice(g * C_GROUP_DIM, (g + 1) * C_GROUP_DIM)
        hg = h_ref[:, cols]
        halo = jnp.where(keep_halo, halo_ref[:, cols], 0.0)
        acc = jnp.concatenate([halo, hg], axis=0)
        span = 1
        while span < window:
            acc = acc + pltpu.roll(acc, span, 0)
            span *= 2
        den = jnp.minimum(pos + 1, window).astype(F32)
        pooled = acc[C_HALO:, :] / den - hg
        y = jnp.dot(pooled.astype(BF16), wg_ref[g], preferred_element_type=F32) * scale_ref[:, cols]
        o_ref[:, cols] = y.astype(o_ref.dtype)


def _pool_mix(h, w_group, scale, seq, *, rows=256):
    m, width = h.shape
    halo_per_block = rows // C_HALO
    return pl.pallas_call(
        functools.partial(_pool_kernel, rows=rows, seq=seq),
        grid=(m // rows,),
        in_specs=[pl.BlockSpec((rows, width), lambda i: (i, 0)),
                  pl.BlockSpec((C_HALO, width), lambda i: (jnp.maximum(i * halo_per_block - 1, 0), 0)),
                  pl.BlockSpec((len(C_WINDOWS), C_GROUP_DIM, C_GROUP_DIM), lambda i: (0, 0, 0)),
                  pl.BlockSpec((1, width), lambda i: (0, 0))],
        out_specs=pl.BlockSpec((rows, width), lambda i: (i, 0)),
        out_shape=jax.ShapeDtypeStruct((m, width), BF16),
        compiler_params=_params("parallel"),
        name="pool_mix",
    )(h, h, w_group, scale)


def _ffn_up_kernel(*refs, tm, nj, tail_width, tiles_per_seq, n_jobs):
    x_ref, wg_ref, wu_ref, cwg_ref, cwu_ref, cbg_ref, cbu_ref = refs[:7]
    o_ref = refs[7 + n_jobs]
    hbuf, carry = refs[-2:]
    i = pl.program_id(0)
    j = pl.program_id(1)
    seq_start = (i % tiles_per_seq) == 0

    @pl.when((i == 0) & (j == 0))
    def _():
        carry[...] = jnp.zeros(carry.shape, F32)

    @pl.when(seq_start)
    def _():
        hbuf[:, 0:CARRY_ROWS, :] = jnp.zeros((2, CARRY_ROWS, FF_BLOCK), F32)

    @pl.when(jnp.logical_not(seq_start))
    def _():
        hbuf[:, 0:CARRY_ROWS, :] = carry[j]

    _run_side_casts(refs[7:7 + n_jobs], refs[8 + n_jobs:-2])

    def block(width):
        x = x_ref[...]
        halves = []
        for s, (w_ref, cw_ref, cb_ref) in enumerate(((wg_ref, cwg_ref, cbg_ref), (wu_ref, cwu_ref, cbu_ref))):
            h = jnp.dot(x, w_ref[:, :width], preferred_element_type=F32)
            hbuf[s, CARRY_ROWS:, :width] = h
            carry[j, s, :, :width] = h[tm - CARRY_ROWS:, :]
            h1 = hbuf[s, CARRY_ROWS - 1:CARRY_ROWS - 1 + tm, :width]
            h2 = hbuf[s, CARRY_ROWS - 2:CARRY_ROWS - 2 + tm, :width]
            halves.append(cw_ref[2:3, :width] * h + cw_ref[1:2, :width] * h1 + cw_ref[0:1, :width] * h2
                          + cb_ref[:, :width])
        gate, up = halves
        o_ref[:, :width] = (gate / (1.0 + jnp.exp(-gate)) * up).astype(o_ref.dtype)

    if tail_width == FF_BLOCK:
        block(FF_BLOCK)
    else:
        pl.when(j < nj - 1)(lambda: block(FF_BLOCK))
        pl.when(j == nj - 1)(lambda: block(tail_width))


def _ffn_up(x, w_gate, w_up, cw_gate, cw_up, cb_gate, cb_up, seq, *, cast_jobs=(), tm=1024):
    m, k = x.shape
    d_ff = w_gate.shape[1]
    nj = pl.cdiv(d_ff, FF_BLOCK)
    col = lambda i, j: (0, j)
    c_in, c_out, c_shapes = _side_cast_plumbing(cast_jobs, (m // tm) * nj, lambda i, j: i * nj + j)
    res = pl.pallas_call(
        functools.partial(_ffn_up_kernel, tm=tm, nj=nj, tail_width=d_ff - (nj - 1) * FF_BLOCK,
                          tiles_per_seq=seq // tm, n_jobs=len(cast_jobs)),
        grid=(m // tm, nj),
        in_specs=[pl.BlockSpec((tm, k), lambda i, j: (i, 0)),
                  pl.BlockSpec((k, FF_BLOCK), col), pl.BlockSpec((k, FF_BLOCK), col),
                  pl.BlockSpec((CONV_WIDTH, FF_BLOCK), col), pl.BlockSpec((CONV_WIDTH, FF_BLOCK), col),
                  pl.BlockSpec((1, FF_BLOCK), col), pl.BlockSpec((1, FF_BLOCK), col)] + c_in,
        out_specs=[pl.BlockSpec((tm, FF_BLOCK), lambda i, j: (i, j))] + c_out,
        out_shape=[jax.ShapeDtypeStruct((m, d_ff), BF16)] + c_shapes,
        scratch_shapes=[pltpu.VMEM((2, tm + CARRY_ROWS, FF_BLOCK), F32),
                        pltpu.VMEM((nj, 2, CARRY_ROWS, FF_BLOCK), F32)],
        compiler_params=_params("arbitrary", "arbitrary"),
        name="ffn_up_conv_gate",
    )(x, w_gate, w_up, cw_gate, cw_up, cb_gate, cb_up, *[job[0] for job in cast_jobs])
    return res[0], res[1:]


def _proj_ln_kernel(*refs, nk, k_tail, n_chunks, n_jobs):
    a_ref, w_ref, x_ref, g_ref, b_ref = refs[:5]
    of_ref, ob_ref = refs[5 + n_jobs:7 + n_jobs]
    mu_ref, rstd_ref = refs[-2:]
    _run_side_casts(refs[5:5 + n_jobs], refs[7 + n_jobs:-2])
    k = pl.program_id(1)
    tm, n = of_ref.shape

    @pl.when(k == 0)
    def _():
        of_ref[...] = jnp.dot(a_ref[...], w_ref[...], preferred_element_type=F32)

    if k_tail == a_ref.shape[1]:
        @pl.when(k > 0)
        def _():
            of_ref[...] += jnp.dot(a_ref[...], w_ref[...], preferred_element_type=F32)
    else:
        @pl.when((k > 0) & (k < nk - 1))
        def _():
            of_ref[...] += jnp.dot(a_ref[...], w_ref[...], preferred_element_type=F32)

        @pl.when(k == nk - 1)
        def _():
            of_ref[...] += jnp.dot(a_ref[:, :k_tail], w_ref[:k_tail, :], preferred_element_type=F32)

    cw = n // n_chunks
    for c in range(n_chunks):
        @pl.when(k == c)
        def _(c=c):
            of_ref[:, c * cw:(c + 1) * cw] += DEEPNORM_ALPHA * x_ref[...]

    @pl.when(k == nk - 1)
    def _():
        steps = tm // LN_ROWS

        def rows_at(r):
            return pl.ds(pl.multiple_of(r * LN_ROWS, LN_ROWS), LN_ROWS)

        def mean_pass(r, carry):
            z = of_ref[rows_at(r), :]
            mu_ref[rows_at(r), :] = jnp.broadcast_to(jnp.mean(z, axis=-1, keepdims=True), (LN_ROWS, LANES))
            return carry

        def across(stat_ref, r):
            return jnp.concatenate([stat_ref[rows_at(r), :]] * (n // LANES), axis=1)

        def var_pass(r, carry):
            d = of_ref[rows_at(r), :] - across(mu_ref, r)
            var = jnp.mean(d * d, axis=-1, keepdims=True)
            rstd_ref[rows_at(r), :] = jnp.broadcast_to(lax.rsqrt(var + LN_EPS), (LN_ROWS, LANES))
            return carry

        def norm_pass(r, carry):
            d = of_ref[rows_at(r), :] - across(mu_ref, r)
            y = d * across(rstd_ref, r) * g_ref[...] + b_ref[...]
            of_ref[rows_at(r), :] = y
            ob_ref[rows_at(r), :] = y.astype(ob_ref.dtype)
            return carry

        lax.fori_loop(0, steps, mean_pass, 0, unroll=16)
        lax.fori_loop(0, steps, var_pass, 0, unroll=16)
        lax.fori_loop(0, steps, norm_pass, 0, unroll=4)


def _proj_ln(a, w, x, g, b, *, name, cast_jobs=(), tm=512, tk=1024):
    m, kdim = a.shape
    n = w.shape[1]
    nk = pl.cdiv(kdim, tk)
    k_tail = kdim - (nk - 1) * tk
    n_chunks = 1
    while n_chunks * 2 <= min(nk, 8):
        n_chunks *= 2
    cw = n // n_chunks
    c_in, c_out, c_shapes = _side_cast_plumbing(cast_jobs, (m // tm) * nk, lambda i, k: i * nk + k)
    res = pl.pallas_call(
        functools.partial(_proj_ln_kernel, nk=nk, k_tail=k_tail, n_chunks=n_chunks, n_jobs=len(cast_jobs)),
        grid=(m // tm, nk),
        in_specs=[pl.BlockSpec((tm, tk), lambda i, k: (i, k)),
                  pl.BlockSpec((tk, n), lambda i, k: (k, 0)),
                  pl.BlockSpec((tm, cw), lambda i, k: (i, jnp.minimum(k, n_chunks - 1))),
                  pl.BlockSpec((LN_ROWS, n), lambda i, k: (0, 0)),
                  pl.BlockSpec((LN_ROWS, n), lambda i, k: (0, 0))] + c_in,
        out_specs=[pl.BlockSpec((tm, n), lambda i, k: (i, 0)),
                   pl.BlockSpec((tm, n), lambda i, k: (i, 0))] + c_out,
        out_shape=[jax.ShapeDtypeStruct((m, n), F32), jax.ShapeDtypeStruct((m, n), BF16)] + c_shapes,
        scratch_shapes=[pltpu.VMEM((tm, LANES), F32), pltpu.VMEM((tm, LANES), F32)],
        compiler_params=_params("arbitrary", "arbitrary"),
        name=name,
    )(a, w, x, jnp.broadcast_to(g.reshape(1, n), (LN_ROWS, n)), jnp.broadcast_to(b.reshape(1, n), (LN_ROWS, n)),
      *[job[0] for job in cast_jobs])
    return res[0], res[1], res[2:]


def _rotary_inv_freq():
    half = B_ROT_DIM // 2
    inv_freq = jnp.float32(ROPE_THETA) ** (-jnp.arange(half, dtype=F32) * 2.0 / B_ROT_DIM)
    zeros = jnp.zeros((B_HEAD_DIM - B_ROT_DIM,), F32)
    return jnp.concatenate([-inv_freq, inv_freq, zeros]).reshape(1, B_HEAD_DIM)


def kernel(x, positions, a_w_in, a_ln_g, a_ln_b, a_w_s, a_b_s, a_w_out, b_w_in, b_w_out, c_w_in, c_w_group, c_scale, c_w_out, f_w_up, f_conv_w, f_conv_b, f_w_down, ln_mix_g, ln_mix_b, ln_ffn_g, ln_ffn_b):
    batch, seq, d = x.shape
    m = batch * seq
    xf = x.reshape(m, d)
    xb = xf.astype(BF16)
    posb = jnp.broadcast_to(positions.reshape(m, 1).astype(F32), (m, B_HEAD_DIM))
    invf = _rotary_inv_freq()

    c_w_group_2d = c_w_group.reshape(c_w_group.shape[0], -1, c_w_group.shape[-1])
    a_in = [a_w_in[0].astype(BF16), None]
    a_out = [None, None]
    ffn_up_w = None
    for layer in range(DEPTH):
        kind, idx = layer % N_MIXERS, layer // N_MIXERS
        if kind == 0:
            jobs = [(f_w_up, 0, True), (a_w_out, 0, False)] if layer == 0 else []
            h, casts = _mm(xb, a_in[idx], epilogue=_gelu, out_dtype=BF16, name="gmlp_in_gelu", cast_jobs=jobs)
            if layer == 0:
                ffn_up_w, a_out[0] = casts[:2], casts[2]
            bs_exp = jnp.repeat(a_b_s[idx].T, A_GROUP_DIM, axis=1)
            y = _spatial_gate(h, a_ln_g[idx].reshape(1, d), a_ln_b[idx].reshape(1, d), a_w_s[idx], bs_exp)
            w_out = a_out[idx]
        elif kind == 1:
            outs, lses, dilations = [], [], []
            for group, (window, dilation) in enumerate(B_PATTERNS):
                assert window // dilation == B_QBLOCK
                qkv = _qkv_proj(xb, b_in, group, posb, invf, dilation)
                o, lse = _dilated_attention(qkv, seq, dilation)
                outs.append(o)
                lses.append(lse)
                dilations.append(dilation)
            y = _combine_groups(outs, lses, tuple(dilations), m)
            w_out = b_out
        else:
            h, _ = _mm(xb, c_in, epilogue=lambda t: t, out_dtype=F32, name="pool_in")
            y = _pool_mix(h, c_group.reshape(c_w_group[idx].shape), c_scale[idx].reshape(1, d), seq)
            w_out = c_out
        xf, xb, _ = _proj_ln(y, w_out, xf, ln_mix_g[layer], ln_mix_b[layer], name="mixer_out_ln")

        jobs = [(f_w_down, layer, False)]
        nxt = layer + 1
        nxt_kind, nxt_idx = nxt % N_MIXERS, nxt // N_MIXERS
        if nxt < DEPTH and nxt_kind == 0:
            jobs += [(a_w_in, nxt_idx, False), (a_w_out, nxt_idx, False)]
        elif nxt < DEPTH and nxt_kind == 1:
            jobs += [(b_w_in, nxt_idx, False), (b_w_out, nxt_idx, False)]
        elif nxt < DEPTH:
            jobs += [(c_w_in, nxt_idx, False), (c_w_out, nxt_idx, False), (c_w_group_2d, nxt_idx, False)]
        cw, cb = f_conv_w[layer], f_conv_b[layer].reshape(1, -1)
        g, casts = _ffn_up(xb, ffn_up_w[0], ffn_up_w[1], cw[:, :D_FF], cw[:, D_FF:], cb[:, :D_FF], cb[:, D_FF:],
                           seq, cast_jobs=jobs)
        w_down = casts[0]
        if nxt < DEPTH and nxt_kind == 0:
            a_in[nxt_idx], a_out[nxt_idx] = casts[1:3]
        elif nxt < DEPTH and nxt_kind == 1:
            b_in, b_out = casts[1:3]
        elif nxt < DEPTH:
            c_in, c_out, c_group = casts[1:4]
        jobs = [(f_w_up, nxt, True)] if nxt < DEPTH else []
        xf, xb, casts = _proj_ln(g, w_down, xf, ln_ffn_g[layer], ln_ffn_b[layer], name="ffn_down_ln",
                                 cast_jobs=jobs)
        if nxt < DEPTH:
            ffn_up_w = casts

    return xf.reshape(batch, seq, d)
```

```python
import functools
import math

import jax
import jax.numpy as jnp
from jax import lax
from jax.experimental import pallas as pl
from jax.experimental.pallas import tpu as pltpu

F32 = jnp.float32
BF16 = jnp.bfloat16

D_MODEL = 4096
DEPTH = 4
N_MIXERS = 3

A_CHUNK = 128
A_GROUPS = 16
A_GROUP_DIM = D_MODEL // A_GROUPS

B_PATTERNS = ((128, 1), (512, 4), (2048, 16))
B_HEADS = 16
B_HEAD_DIM = 128
B_ROT_DIM = B_HEAD_DIM // 4
B_QBLOCK = 128
B_WIDTH = B_HEADS * B_HEAD_DIM
ROPE_THETA = 500000.0

C_WINDOWS = (2, 4, 8, 16)
C_GROUP_DIM = D_MODEL // len(C_WINDOWS)
C_HALO = 16

D_FF = 11008
FF_BLOCK = 768
CONV_WIDTH = 3
CARRY_ROWS = 8

LN_EPS = 1e-5
LN_ROWS = 8
DEEPNORM_ALPHA = (2 * DEPTH) ** 0.25

LANES = 128
V7X_VMEM_LIMIT_BYTES = 63 * 1024 * 1024


def _params(*semantics, flags=None):
    return pltpu.CompilerParams(dimension_semantics=semantics, vmem_limit_bytes=V7X_VMEM_LIMIT_BYTES,
                                flags=flags)


def _layer_norm(z, g, b):
    mu = jnp.mean(z, axis=-1, keepdims=True)
    d = z - mu
    var = jnp.mean(d * d, axis=-1, keepdims=True)
    return d * lax.rsqrt(var + LN_EPS) * g + b


def _cast_slab_rows(total_rows, n_steps, smallest=16):
    rows = smallest
    while total_rows % rows or total_rows // rows > n_steps:
        rows *= 2
        assert rows <= total_rows
    return rows


def _side_cast_plumbing(jobs, n_steps, step_of):
    in_specs, out_specs, out_shapes = [], [], []
    for src, layer, split in jobs:
        _, rows, width = src.shape
        slab = _cast_slab_rows(rows, n_steps)
        slab_of = lambda *g, last=rows // slab - 1: jnp.minimum(step_of(*g), last)
        in_specs.append(pl.BlockSpec((None, slab, width), lambda *g, f=slab_of, l=layer: (l, f(*g), 0)))
        parts = 2 if split else 1
        out_specs += [pl.BlockSpec((slab, width // parts), lambda *g, f=slab_of: (f(*g), 0))] * parts
        out_shapes += [jax.ShapeDtypeStruct((rows, width // parts), BF16)] * parts
    return in_specs, out_specs, out_shapes


def _run_side_casts(src_refs, dst_refs):
    dst = list(dst_refs)
    for src in src_refs:
        width = dst[0].shape[1]
        for part in range(src.shape[1] // width):
            dst.pop(0)[...] = src[:, part * width:(part + 1) * width].astype(BF16)
    assert not dst


def _gelu(x):
    return 0.5 * x * (1.0 + lax.erf(x * (1.0 / math.sqrt(2.0))))


def _mm_kernel(*refs, epilogue, n_jobs):
    x_ref, w_ref = refs[:2]
    o_ref = refs[2 + n_jobs]
    _run_side_casts(refs[2:2 + n_jobs], refs[3 + n_jobs:])
    acc = jnp.dot(x_ref[...], w_ref[...], preferred_element_type=F32)
    o_ref[...] = epilogue(acc).astype(o_ref.dtype)


def _mm(x, w, *, epilogue, out_dtype, name, cast_jobs=(), tm=1024, tn=1024):
    m, k = x.shape
    n = w.shape[1]
    nj = n // tn
    c_in, c_out, c_shapes = _side_cast_plumbing(cast_jobs, (m // tm) * nj, lambda i, j: i * nj + j)
    res = pl.pallas_call(
        functools.partial(_mm_kernel, epilogue=epilogue, n_jobs=len(cast_jobs)),
        grid=(m // tm, nj),
        in_specs=[pl.BlockSpec((tm, k), lambda i, j: (i, 0)),
                  pl.BlockSpec((k, tn), lambda i, j: (0, j))] + c_in,
        out_specs=[pl.BlockSpec((tm, tn), lambda i, j: (i, j))] + c_out,
        out_shape=[jax.ShapeDtypeStruct((m, n), out_dtype)] + c_shapes,
        compiler_params=_params("arbitrary", "arbitrary"),
        name=name,
    )(x, w, *[job[0] for job in cast_jobs])
    return res[0], res[1:]


def _qkv_kernel(x_ref, w_ref, pos_ref, invf_ref, o_ref, cos_ref, sin_ref, *perm_scratch,
                tm, tn, dilation):
    j = pl.program_id(1)

    @pl.when(j == 0)
    def _():
        ang = pos_ref[...] * invf_ref[...]
        cos_ref[...] = jnp.cos(ang)
        sin_ref[...] = jnp.sin(ang)

    def block(rotate):
        out = jnp.dot(x_ref[...], w_ref[...], preferred_element_type=F32)
        if rotate:
            reps = tn // B_HEAD_DIM
            cos = jnp.concatenate([cos_ref[...]] * reps, axis=1)
            sin = jnp.concatenate([sin_ref[...]] * reps, axis=1)
            lane = lax.broadcasted_iota(jnp.int32, out.shape, 1) & (B_HEAD_DIM - 1)
            half = B_ROT_DIM // 2
            partner = jnp.where(lane < half,
                                pltpu.roll(out, tn - half, 1),
                                pltpu.roll(out, half, 1))
            out = out * cos + partner * sin

        chunk = B_QBLOCK * dilation
        if dilation == 1:
            for c in range(tm // chunk):
                o_ref[c, 0] = out[c * chunk:(c + 1) * chunk].astype(o_ref.dtype)
            return
        (buf,) = perm_scratch
        for t in range(tn // LANES):
            cols = slice(t * LANES, (t + 1) * LANES)
            buf[t] = out[:, cols]
            if chunk <= tm:
                for c in range(tm // chunk):
                    for r in range(dilation):
                        rows = pl.ds(c * chunk + r, B_QBLOCK, stride=dilation)
                        o_ref[c, r, :, cols] = buf[t, rows, :].astype(o_ref.dtype)
            else:
                for r in range(dilation):
                    rows = pl.ds(r, tm // dilation, stride=dilation)
                    o_ref[r, :, cols] = buf[t, rows, :].astype(o_ref.dtype)

    is_v = (j * tn) // B_WIDTH == 2
    pl.when(jnp.logical_not(is_v))(lambda: block(True))
    pl.when(is_v)(lambda: block(False))


def _qkv_proj(x, w, group, posb, invf, dilation, *, tm=1024, tn=1024):
    m, k = x.shape
    n = 3 * B_WIDTH
    first = group * n // tn
    chunk = B_QBLOCK * dilation
    if chunk <= tm:
        out_spec = pl.BlockSpec((tm // chunk, dilation, B_QBLOCK, tn), lambda i, j: (i, 0, 0, j))
    else:
        tiles_per_chunk = chunk // tm
        out_spec = pl.BlockSpec((None, dilation, tm // dilation, tn),
                                lambda i, j: (i // tiles_per_chunk, 0, i % tiles_per_chunk, j))
    scratch = [pltpu.VMEM((tm, B_HEAD_DIM), F32), pltpu.VMEM((tm, B_HEAD_DIM), F32)]
    if dilation > 1:
        scratch.append(pltpu.VMEM((tn // LANES, tm, LANES), F32))
    return pl.pallas_call(
        functools.partial(_qkv_kernel, tm=tm, tn=tn, dilation=dilation),
        grid=(m // tm, n // tn),
        in_specs=[pl.BlockSpec((tm, k), lambda i, j: (i, 0)),
                  pl.BlockSpec((k, tn), lambda i, j: (0, first + j)),
                  pl.BlockSpec((tm, B_HEAD_DIM), lambda i, j: (i, 0)),
                  pl.BlockSpec((1, B_HEAD_DIM), lambda i, j: (0, 0))],
        out_specs=out_spec,
        out_shape=jax.ShapeDtypeStruct((m // chunk, dilation, B_QBLOCK, n), BF16),
        scratch_shapes=scratch,
        compiler_params=_params("arbitrary", "arbitrary"),
        name=f"qkv_proj_d{dilation}",
    )(x, w, posb, invf)


def _attn_kernel(q_ref, kp_ref, kc_ref, vp_ref, vc_ref, o_ref, l_ref, *, chunks_per_seq):
    c = pl.program_id(0)
    has_prev = (c % chunks_per_seq) > 0
    q_blk = B_QBLOCK
    qi = lax.broadcasted_iota(jnp.int32, (q_blk, 2 * q_blk), 0)
    kj = lax.broadcasted_iota(jnp.int32, (q_blk, 2 * q_blk), 1)
    mask = (kj >= qi) & (kj <= qi + q_blk) & ((kj >= q_blk) | has_prev)
    scale = B_HEAD_DIM ** -0.5
    head_lane = lax.broadcasted_iota(jnp.int32, (q_blk, LANES), 1)
    lse_tile = jnp.zeros((q_blk, LANES), F32)
    for h in range(B_HEADS):
        sl = slice(h * B_HEAD_DIM, (h + 1) * B_HEAD_DIM)
        q = q_ref[:, sl]
        k2 = jnp.concatenate([kp_ref[:, sl], kc_ref[:, sl]], axis=0)
        v2 = jnp.concatenate([vp_ref[:, sl], vc_ref[:, sl]], axis=0)
        s = lax.dot_general(q, k2, (((1,), (1,)), ((), ())), preferred_element_type=F32) * scale
        s = jnp.where(mask, s, -jnp.inf)
        mx = jnp.max(s, axis=-1, keepdims=True)
        p = jnp.exp(s - mx)
        den = jnp.sum(p, axis=-1, keepdims=True)
        o = jnp.dot(p.astype(BF16), v2, preferred_element_type=F32) / den
        o_ref[:, sl] = o.astype(o_ref.dtype)
        lse_tile = jnp.where(head_lane == h, mx + jnp.log(den), lse_tile)
    l_ref[...] = lse_tile


def _dilated_attention(qkv, seq, dilation):
    n_chunks = qkv.shape[0]
    chunks_per_seq = seq // (B_QBLOCK * dilation)

    def spec(which, prev):
        def index(c, r):
            return (jnp.maximum(c - 1, 0) if prev else c, r, 0, which)
        return pl.BlockSpec((None, None, B_QBLOCK, B_WIDTH), index)

    return pl.pallas_call(
        functools.partial(_attn_kernel, chunks_per_seq=chunks_per_seq),
        grid=(n_chunks, dilation),
        in_specs=[spec(0, False), spec(1, True), spec(1, False), spec(2, True), spec(2, False)],
        out_specs=[pl.BlockSpec((None, None, B_QBLOCK, B_WIDTH), lambda c, r: (c, r, 0, 0)),
                   pl.BlockSpec((None, None, B_QBLOCK, LANES), lambda c, r: (c, r, 0, 0))],
        out_shape=[jax.ShapeDtypeStruct((n_chunks, dilation, B_QBLOCK, B_WIDTH), BF16),
                   jax.ShapeDtypeStruct((n_chunks, dilation, B_QBLOCK, LANES), F32)],
        compiler_params=_params("parallel", "parallel"),
        name=f"dilated_attn_d{dilation}",
    )(qkv, qkv, qkv, qkv, qkv)


def _combine_kernel(*refs, tm, dilations):
    n = len(dilations)
    o_refs, l_refs, out_ref = refs[:n], refs[n:2 * n], refs[2 * n]
    o_bufs, l_bufs = refs[2 * n + 1:3 * n + 1], refs[3 * n + 1:4 * n + 1]

    def scatter(src_o, src_l, o_buf, l_buf, rows):
        l_buf[rows, :] = src_l
        for h in range(B_HEADS):
            o_buf[h, rows, :] = src_o[:, h * B_HEAD_DIM:(h + 1) * B_HEAD_DIM].astype(F32)

    for o_ref, l_ref, o_buf, l_buf, d in zip(o_refs, l_refs, o_bufs, l_bufs, dilations):
        chunk = B_QBLOCK * d
        if chunk <= tm:
            for c in range(tm // chunk):
                for r in range(d):
                    rows = pl.ds(c * chunk + r, B_QBLOCK, stride=d) if d > 1 else pl.ds(c * chunk, B_QBLOCK)
                    scatter(o_ref[c, r], l_ref[c, r], o_buf, l_buf, rows)
        else:
            for r in range(d):
                scatter(o_ref[r], l_ref[r], o_buf, l_buf, pl.ds(r, tm // d, stride=d))

    lses = [l_buf[...] for l_buf in l_bufs]
    mx = functools.reduce(jnp.maximum, lses)
    es = [jnp.exp(l - mx) for l in lses]
    tot = functools.reduce(lambda a, b: a + b, es)
    ws = [e / tot for e in es]
    for h in range(B_HEADS):
        acc = None
        for w, o_buf in zip(ws, o_bufs):
            term = jnp.broadcast_to(w[:, h:h + 1], (tm, B_HEAD_DIM)) * o_buf[h]
            acc = term if acc is None else acc + term
        out_ref[:, h * B_HEAD_DIM:(h + 1) * B_HEAD_DIM] = acc.astype(out_ref.dtype)


def _combine_groups(outs, lses, dilations, m, *, tm=512):
    def spec(d, width):
        chunk = B_QBLOCK * d
        if chunk <= tm:
            return pl.BlockSpec((tm // chunk, d, B_QBLOCK, width), lambda i: (i, 0, 0, 0))
        tiles_per_chunk = chunk // tm
        return pl.BlockSpec((None, d, tm // d, width),
                            lambda i: (i // tiles_per_chunk, 0, i % tiles_per_chunk, 0))

    n = len(dilations)
    return pl.pallas_call(
        functools.partial(_combine_kernel, tm=tm, dilations=dilations),
        grid=(m // tm,),
        in_specs=[spec(d, B_WIDTH) for d in dilations] + [spec(d, LANES) for d in dilations],
        out_specs=pl.BlockSpec((tm, B_WIDTH), lambda i: (i, 0)),
        out_shape=jax.ShapeDtypeStruct((m, B_WIDTH), BF16),
        scratch_shapes=[pltpu.VMEM((B_HEADS, tm, B_HEAD_DIM), F32)] * n + [pltpu.VMEM((tm, LANES), F32)] * n,
        compiler_params=_params("parallel"),
        name="attn_combine",
    )(*outs, *lses)


def _gate_kernel(u_ref, v_ref, g_ref, b_ref, ws_ref, bs_ref, o_ref, *, rows):
    vn = _layer_norm(v_ref[...].astype(F32), g_ref[...], b_ref[...]).astype(BF16)
    ti = lax.broadcasted_iota(jnp.int32, (A_CHUNK, A_CHUNK), 0)
    si = lax.broadcasted_iota(jnp.int32, (A_CHUNK, A_CHUNK), 1)
    causal = si <= ti
    for g in range(A_GROUPS):
        cols = slice(g * A_GROUP_DIM, (g + 1) * A_GROUP_DIM)
        w = jnp.where(causal, ws_ref[g], 0.0).astype(BF16)
        bias = bs_ref[:, cols]
        for c in range(rows // A_CHUNK):
            rws = slice(c * A_CHUNK, (c + 1) * A_CHUNK)
            mixed = jnp.dot(w, vn[rws, cols], preferred_element_type=F32) + bias
            o_ref[rws, cols] = (u_ref[rws, cols].astype(F32) * mixed).astype(o_ref.dtype)


def _spatial_gate(h, ln_g, ln_b, w_s, bs_exp, *, rows=256):
    m = h.shape[0]
    width = D_MODEL
    return pl.pallas_call(
        functools.partial(_gate_kernel, rows=rows),
        grid=(m // rows,),
        in_specs=[pl.BlockSpec((rows, width), lambda i: (i, 0)),
                  pl.BlockSpec((rows, width), lambda i: (i, 1)),
                  pl.BlockSpec((1, width), lambda i: (0, 0)),
                  pl.BlockSpec((1, width), lambda i: (0, 0)),
                  pl.BlockSpec((A_GROUPS, A_CHUNK, A_CHUNK), lambda i: (0, 0, 0)),
                  pl.BlockSpec((A_CHUNK, width), lambda i: (0, 0))],
        out_specs=pl.BlockSpec((rows, width), lambda i: (i, 0)),
        out_shape=jax.ShapeDtypeStruct((m, width), BF16),
        compiler_params=_params("parallel"),
        name="spatial_gate",
    )(h, h, ln_g, ln_b, w_s, bs_exp)


def _pool_kernel(h_ref, halo_ref, wg_ref, scale_ref, o_ref, *, rows, seq):
    i = pl.program_id(0)
    t0 = (i * rows) % seq
    pos = t0 + lax.broadcasted_iota(jnp.int32, (rows, 1), 0)
    keep_halo = t0 > 0
    for g, window in enumerate(C_WINDOWS):
        cols = slice(g * C_GROUP_DIM, (g + 1) * C_GROUP_DIM)
        hg = h_ref[:, cols]
        halo = jnp.where(keep_halo, halo_ref[:, cols], 0.0)
        acc = jnp.concatenate([halo, hg], axis=0)
        span = 1
        while span < window:
            acc = acc + pltpu.roll(acc, span, 0)
            span *= 2
        den = jnp.minimum(pos + 1, window).astype(F32)
        pooled = acc[C_HALO:, :] / den - hg
        y = jnp.dot(pooled.astype(BF16), wg_ref[g], preferred_element_type=F32) * scale_ref[:, cols]
        o_ref[:, cols] = y.astype(o_ref.dtype)


def _pool_mix(h, w_group, scale, seq, *, rows=256):
    m, width = h.shape
    halo_per_block = rows // C_HALO
    return pl.pallas_call(
        functools.partial(_pool_kernel, rows=rows, seq=seq),
        grid=(m // rows,),
        in_specs=[pl.BlockSpec((rows, width), lambda i: (i, 0)),
                  pl.BlockSpec((C_HALO, width), lambda i: (jnp.maximum(i * halo_per_block - 1, 0), 0)),
                  pl.BlockSpec((len(C_WINDOWS), C_GROUP_DIM, C_GROUP_DIM), lambda i: (0, 0, 0)),
                  pl.BlockSpec((1, width), lambda i: (0, 0))],
        out_specs=pl.BlockSpec((rows, width), lambda i: (i, 0)),
        out_shape=jax.ShapeDtypeStruct((m, width), BF16),
        compiler_params=_params("parallel"),
        name="pool_mix",
    )(h, h, w_group, scale)


def _ffn_up_kernel(*refs, tm, nj, tail_width, tiles_per_seq, n_jobs):
    x_ref, wg_ref, wu_ref, cwg_ref, cwu_ref, cbg_ref, cbu_ref = refs[:7]
    o_ref = refs[7 + n_jobs]
    hbuf, carry = refs[-2:]
    i = pl.program_id(0)
    j = pl.program_id(1)
    seq_start = (i % tiles_per_seq) == 0

    @pl.when((i == 0) & (j == 0))
    def _():
        carry[...] = jnp.zeros(carry.shape, F32)

    @pl.when(seq_start)
    def _():
        hbuf[:, 0:CARRY_ROWS, :] = jnp.zeros((2, CARRY_ROWS, FF_BLOCK), F32)

    @pl.when(jnp.logical_not(seq_start))
    def _():
        hbuf[:, 0:CARRY_ROWS, :] = carry[j]

    _run_side_casts(refs[7:7 + n_jobs], refs[8 + n_jobs:-2])

    def block(width):
        x = x_ref[...]
        halves = []
        for s, (w_ref, cw_ref, cb_ref) in enumerate(((wg_ref, cwg_ref, cbg_ref), (wu_ref, cwu_ref, cbu_ref))):
            h = jnp.dot(x, w_ref[:, :width], preferred_element_type=F32)
            hbuf[s, CARRY_ROWS:, :width] = h
            carry[j, s, :, :width] = h[tm - CARRY_ROWS:, :]
            h1 = hbuf[s, CARRY_ROWS - 1:CARRY_ROWS - 1 + tm, :width]
            h2 = hbuf[s, CARRY_ROWS - 2:CARRY_ROWS - 2 + tm, :width]
            halves.append(cw_ref[2:3, :width] * h + cw_ref[1:2, :width] * h1 + cw_ref[0:1, :width] * h2
                          + cb_ref[:, :width])
        gate, up = halves
        o_ref[:, :width] = (gate / (1.0 + jnp.exp(-gate)) * up).astype(o_ref.dtype)

    if tail_width == FF_BLOCK:
        block(FF_BLOCK)
    else:
        pl.when(j < nj - 1)(lambda: block(FF_BLOCK))
        pl.when(j == nj - 1)(lambda: block(tail_width))


def _ffn_up(x, w_gate, w_up, cw_gate, cw_up, cb_gate, cb_up, seq, *, cast_jobs=(), tm=1024):
    m, k = x.shape
    d_ff = w_gate.shape[1]
    nj = pl.cdiv(d_ff, FF_BLOCK)
    col = lambda i, j: (0, j)
    c_in, c_out, c_shapes = _side_cast_plumbing(cast_jobs, (m // tm) * nj, lambda i, j: i * nj + j)
    res = pl.pallas_call(
        functools.partial(_ffn_up_kernel, tm=tm, nj=nj, tail_width=d_ff - (nj - 1) * FF_BLOCK,
                          tiles_per_seq=seq // tm, n_jobs=len(cast_jobs)),
        grid=(m // tm, nj),
        in_specs=[pl.BlockSpec((tm, k), lambda i, j: (i, 0), pipeline_mode=pl.Buffered(1)),
                  pl.BlockSpec((k, FF_BLOCK), col), pl.BlockSpec((k, FF_BLOCK), col),
                  pl.BlockSpec((CONV_WIDTH, FF_BLOCK), col), pl.BlockSpec((CONV_WIDTH, FF_BLOCK), col),
                  pl.BlockSpec((1, FF_BLOCK), col), pl.BlockSpec((1, FF_BLOCK), col)] + c_in,
        out_specs=[pl.BlockSpec((tm, FF_BLOCK), lambda i, j: (i, j))] + c_out,
        out_shape=[jax.ShapeDtypeStruct((m, d_ff), BF16)] + c_shapes,
        scratch_shapes=[pltpu.VMEM((2, tm + CARRY_ROWS, FF_BLOCK), F32),
                        pltpu.VMEM((nj, 2, CARRY_ROWS, FF_BLOCK), F32)],
        compiler_params=_params("arbitrary", "arbitrary"),
        name="ffn_up_conv_gate",
    )(x, w_gate, w_up, cw_gate, cw_up, cb_gate, cb_up, *[job[0] for job in cast_jobs])
    return res[0], res[1:]


def _proj_ln_kernel(*refs, nk, k_tail, n_chunks, n_jobs):
    a_ref, w_ref, x_ref, g_ref, b_ref = refs[:5]
    of_ref, ob_ref = refs[5 + n_jobs:7 + n_jobs]
    mu_ref, rstd_ref = refs[-2:]
    _run_side_casts(refs[5:5 + n_jobs], refs[7 + n_jobs:-2])
    k = pl.program_id(1)
    tm, n = of_ref.shape

    @pl.when(k == 0)
    def _():
        of_ref[...] = jnp.dot(a_ref[...], w_ref[...], preferred_element_type=F32)

    if k_tail == a_ref.shape[1]:
        @pl.when(k > 0)
        def _():
            of_ref[...] += jnp.dot(a_ref[...], w_ref[...], preferred_element_type=F32)
    else:
        @pl.when((k > 0) & (k < nk - 1))
        def _():
            of_ref[...] += jnp.dot(a_ref[...], w_ref[...], preferred_element_type=F32)

        @pl.when(k == nk - 1)
        def _():
            of_ref[...] += jnp.dot(a_ref[:, :k_tail], w_ref[:k_tail, :], preferred_element_type=F32)

    cw = n // n_chunks
    for c in range(n_chunks):
        @pl.when(k == c)
        def _(c=c):
            of_ref[:, c * cw:(c + 1) * cw] += DEEPNORM_ALPHA * x_ref[...]

    @pl.when(k == nk - 1)
    def _():
        steps = tm // LN_ROWS

        def rows_at(r):
            return pl.ds(pl.multiple_of(r * LN_ROWS, LN_ROWS), LN_ROWS)

        def mean_pass(r, carry):
            z = of_ref[rows_at(r), :]
            mu_ref[rows_at(r), :] = jnp.broadcast_to(jnp.mean(z, axis=-1, keepdims=True), (LN_ROWS, LANES))
            return carry

        def across(stat_ref, r):
            return jnp.concatenate([stat_ref[rows_at(r), :]] * (n // LANES), axis=1)

        def var_pass(r, carry):
            d = of_ref[rows_at(r), :] - across(mu_ref, r)
            var = jnp.mean(d * d, axis=-1, keepdims=True)
            rstd_ref[rows_at(r), :] = jnp.broadcast_to(lax.rsqrt(var + LN_EPS), (LN_ROWS, LANES))
            return carry

        def norm_pass(r, carry):
            d = of_ref[rows_at(r), :] - across(mu_ref, r)
            y = d * across(rstd_ref, r) * g_ref[...] + b_ref[...]
            of_ref[rows_at(r), :] = y
            ob_ref[rows_at(r), :] = y.astype(ob_ref.dtype)
            return carry

        lax.fori_loop(0, steps, mean_pass, 0, unroll=16)
        lax.fori_loop(0, steps, var_pass, 0, unroll=16)
        lax.fori_loop(0, steps, norm_pass, 0, unroll=4)


def _proj_ln(a, w, x, g, b, *, name, cast_jobs=(), tm=512, tk=1024):
    m, kdim = a.shape
    n = w.shape[1]
    nk = pl.cdiv(kdim, tk)
    k_tail = kdim - (nk - 1) * tk
    n_chunks = 1
    while n_chunks * 2 <= min(nk, 8):
        n_chunks *= 2
    cw = n // n_chunks
    c_in, c_out, c_shapes = _side_cast_plumbing(cast_jobs, (m // tm) * nk, lambda i, k: i * nk + k)
    res = pl.pallas_call(
        functools.partial(_proj_ln_kernel, nk=nk, k_tail=k_tail, n_chunks=n_chunks, n_jobs=len(cast_jobs)),
        grid=(m // tm, nk),
        in_specs=[pl.BlockSpec((tm, tk), lambda i, k: (i, k)),
                  pl.BlockSpec((tk, n), lambda i, k: (k, 0)),
                  pl.BlockSpec((tm, cw), lambda i, k: (i, jnp.minimum(k, n_chunks - 1))),
                  pl.BlockSpec((LN_ROWS, n), lambda i, k: (0, 0)),
                  pl.BlockSpec((LN_ROWS, n), lambda i, k: (0, 0))] + c_in,
        out_specs=[pl.BlockSpec((tm, n), lambda i, k: (i, 0)),
                   pl.BlockSpec((tm, n), lambda i, k: (i, 0))] + c_out,
        out_shape=[jax.ShapeDtypeStruct((m, n), F32), jax.ShapeDtypeStruct((m, n), BF16)] + c_shapes,
        scratch_shapes=[pltpu.VMEM((tm, LANES), F32), pltpu.VMEM((tm, LANES), F32)],
        compiler_params=_params("arbitrary", "arbitrary"),
        name=name,
    )(a, w, x, jnp.broadcast_to(g.reshape(1, n), (LN_ROWS, n)), jnp.broadcast_to(b.reshape(1, n), (LN_ROWS, n)),
      *[job[0] for job in cast_jobs])
    return res[0], res[1], res[2:]


def _rotary_inv_freq():
    half = B_ROT_DIM // 2
    inv_freq = jnp.float32(ROPE_THETA) ** (-jnp.arange(half, dtype=F32) * 2.0 / B_ROT_DIM)
    zeros = jnp.zeros((B_HEAD_DIM - B_ROT_DIM,), F32)
    return jnp.concatenate([-inv_freq, inv_freq, zeros]).reshape(1, B_HEAD_DIM)


def kernel(x, positions, a_w_in, a_ln_g, a_ln_b, a_w_s, a_b_s, a_w_out, b_w_in, b_w_out, c_w_in, c_w_group, c_scale, c_w_out, f_w_up, f_conv_w, f_conv_b, f_w_down, ln_mix_g, ln_mix_b, ln_ffn_g, ln_ffn_b):
    batch, seq, d = x.shape
    m = batch * seq
    xf = x.reshape(m, d)
    xb = xf.astype(BF16)
    posb = jnp.broadcast_to(positions.reshape(m, 1).astype(F32), (m, B_HEAD_DIM))
    invf = _rotary_inv_freq()

    c_w_group_2d = c_w_group.reshape(c_w_group.shape[0], -1, c_w_group.shape[-1])
    a_in = [a_w_in[0].astype(BF16), None]
    a_out = [None, None]
    ffn_up_w = None
    for layer in range(DEPTH):
        kind, idx = layer % N_MIXERS, layer // N_MIXERS
        if kind == 0:
            jobs = [(f_w_up, 0, True), (a_w_out, 0, False)] if layer == 0 else []
            h, casts = _mm(xb, a_in[idx], epilogue=_gelu, out_dtype=BF16, name="gmlp_in_gelu", cast_jobs=jobs)
            if layer == 0:
                ffn_up_w, a_out[0] = casts[:2], casts[2]
            bs_exp = jnp.repeat(a_b_s[idx].T, A_GROUP_DIM, axis=1)
            y = _spatial_gate(h, a_ln_g[idx].reshape(1, d), a_ln_b[idx].reshape(1, d), a_w_s[idx], bs_exp)
            w_out = a_out[idx]
        elif kind == 1:
            outs, lses, dilations = [], [], []
            for group, (window, dilation) in enumerate(B_PATTERNS):
                assert window // dilation == B_QBLOCK
                qkv = _qkv_proj(xb, b_in, group, posb, invf, dilation)
                o, lse = _dilated_attention(qkv, seq, dilation)
                outs.append(o)
                lses.append(lse)
                dilations.append(dilation)
            y = _combine_groups(outs, lses, tuple(dilations), m)
            w_out = b_out
        else:
            h, _ = _mm(xb, c_in, epilogue=lambda t: t, out_dtype=F32, name="pool_in")
            y = _pool_mix(h, c_group.reshape(c_w_group[idx].shape), c_scale[idx].reshape(1, d), seq)
            w_out = c_out
        xf, xb, _ = _proj_ln(y, w_out, xf, ln_mix_g[layer], ln_mix_b[layer], name="mixer_out_ln")

        jobs = [(f_w_down, layer, False)]
        nxt = layer + 1
        nxt_kind, nxt_idx = nxt % N_MIXERS, nxt // N_MIXERS
        if nxt < DEPTH and nxt_kind == 0:
            jobs += [(a_w_in, nxt_idx, False), (a_w_out, nxt_idx, False)]
        elif nxt < DEPTH and nxt_kind == 1:
            jobs += [(b_w_in, nxt_idx, False), (b_w_out, nxt_idx, False)]
        elif nxt < DEPTH:
            jobs += [(c_w_in, nxt_idx, False), (c_w_out, nxt_idx, False), (c_w_group_2d, nxt_idx, False)]
        cw, cb = f_conv_w[layer], f_conv_b[layer].reshape(1, -1)
        g, casts = _ffn_up(xb, ffn_up_w[0], ffn_up_w[1], cw[:, :D_FF], cw[:, D_FF:], cb[:, :D_FF], cb[:, D_FF:],
                           seq, cast_jobs=jobs)
        w_down = casts[0]
        if nxt < DEPTH and nxt_kind == 0:
            a_in[nxt_idx], a_out[nxt_idx] = casts[1:3]
        elif nxt < DEPTH and nxt_kind == 1:
            b_in, b_out = casts[1:3]
        elif nxt < DEPTH:
            c_in, c_out, c_group = casts[1:4]
        jobs = [(f_w_up, nxt, True)] if nxt < DEPTH else []
        xf, xb, casts = _proj_ln(g, w_down, xf, ln_ffn_g[layer], ln_ffn_b[layer], name="ffn_down_ln",
                                 cast_jobs=jobs)
        if nxt < DEPTH:
            ffn_up_w = casts

    return xf.reshape(batch, seq, d)
```

```python
import functools
import math

import jax
import jax.numpy as jnp
from jax import lax
from jax.experimental import pallas as pl
from jax.experimental.pallas import tpu as pltpu

F32 = jnp.float32
BF16 = jnp.bfloat16

D_MODEL = 4096
DEPTH = 4
N_MIXERS = 3

A_CHUNK = 128
A_GROUPS = 16
A_GROUP_DIM = D_MODEL // A_GROUPS

B_PATTERNS = ((128, 1), (512, 4), (2048, 16))
B_HEADS = 16
B_HEAD_DIM = 128
B_ROT_DIM = B_HEAD_DIM // 4
B_QBLOCK = 128
B_WIDTH = B_HEADS * B_HEAD_DIM
ROPE_THETA = 500000.0

C_WINDOWS = (2, 4, 8, 16)
C_GROUP_DIM = D_MODEL // len(C_WINDOWS)
C_HALO = 16

D_FF = 11008
FF_BLOCK = 768
CONV_WIDTH = 3
CARRY_ROWS = 8

LN_EPS = 1e-5
LN_ROWS = 8
DEEPNORM_ALPHA = (2 * DEPTH) ** 0.25

LANES = 128
V7X_VMEM_LIMIT_BYTES = 63 * 1024 * 1024


def _params(*semantics, flags=None):
    return pltpu.CompilerParams(dimension_semantics=semantics, vmem_limit_bytes=V7X_VMEM_LIMIT_BYTES,
                                flags=flags)


def _layer_norm(z, g, b):
    mu = jnp.mean(z, axis=-1, keepdims=True)
    d = z - mu
    var = jnp.mean(d * d, axis=-1, keepdims=True)
    return d * lax.rsqrt(var + LN_EPS) * g + b


def _cast_slab_rows(total_rows, n_steps, smallest=16):
    rows = smallest
    while total_rows % rows or total_rows // rows > n_steps:
        rows *= 2
        assert rows <= total_rows
    return rows


def _side_cast_plumbing(jobs, n_steps, step_of):
    in_specs, out_specs, out_shapes = [], [], []
    for src, layer, split in jobs:
        _, rows, width = src.shape
        slab = _cast_slab_rows(rows, n_steps)
        slab_of = lambda *g, last=rows // slab - 1: jnp.minimum(step_of(*g), last)
        in_specs.append(pl.BlockSpec((None, slab, width), lambda *g, f=slab_of, l=layer: (l, f(*g), 0)))
        parts = 2 if split else 1
        out_specs += [pl.BlockSpec((slab, width // parts), lambda *g, f=slab_of: (f(*g), 0))] * parts
        out_shapes += [jax.ShapeDtypeStruct((rows, width // parts), BF16)] * parts
    return in_specs, out_specs, out_shapes


def _run_side_casts(src_refs, dst_refs):
    dst = list(dst_refs)
    for src in src_refs:
        width = dst[0].shape[1]
        for part in range(src.shape[1] // width):
            dst.pop(0)[...] = src[:, part * width:(part + 1) * width].astype(BF16)
    assert not dst


def _gelu(x):
    return 0.5 * x * (1.0 + lax.erf(x * (1.0 / math.sqrt(2.0))))


def _mm_kernel(*refs, epilogue, n_jobs):
    x_ref, w_ref = refs[:2]
    o_ref = refs[2 + n_jobs]
    _run_side_casts(refs[2:2 + n_jobs], refs[3 + n_jobs:])
    acc = jnp.dot(x_ref[...], w_ref[...], preferred_element_type=F32)
    o_ref[...] = epilogue(acc).astype(o_ref.dtype)


def _mm(x, w, *, epilogue, out_dtype, name, cast_jobs=(), tm=1024, tn=1024):
    m, k = x.shape
    n = w.shape[1]
    nj = n // tn
    c_in, c_out, c_shapes = _side_cast_plumbing(cast_jobs, (m // tm) * nj, lambda i, j: i * nj + j)
    res = pl.pallas_call(
        functools.partial(_mm_kernel, epilogue=epilogue, n_jobs=len(cast_jobs)),
        grid=(m // tm, nj),
        in_specs=[pl.BlockSpec((tm, k), lambda i, j: (i, 0)),
                  pl.BlockSpec((k, tn), lambda i, j: (0, j))] + c_in,
        out_specs=[pl.BlockSpec((tm, tn), lambda i, j: (i, j))] + c_out,
        out_shape=[jax.ShapeDtypeStruct((m, n), out_dtype)] + c_shapes,
        compiler_params=_params("arbitrary", "arbitrary"),
        name=name,
    )(x, w, *[job[0] for job in cast_jobs])
    return res[0], res[1:]


def _qkv_kernel(x_ref, w_ref, pos_ref, invf_ref, o_ref, cos_ref, sin_ref, *perm_scratch,
                tm, tn, dilation):
    j = pl.program_id(1)

    @pl.when(j == 0)
    def _():
        ang = pos_ref[...] * invf_ref[...]
        cos_ref[...] = jnp.cos(ang)
        sin_ref[...] = jnp.sin(ang)

    def block(rotate):
        out = jnp.dot(x_ref[...], w_ref[...], preferred_element_type=F32)
        if rotate:
            reps = tn // B_HEAD_DIM
            cos = jnp.concatenate([cos_ref[...]] * reps, axis=1)
            sin = jnp.concatenate([sin_ref[...]] * reps, axis=1)
            lane = lax.broadcasted_iota(jnp.int32, out.shape, 1) & (B_HEAD_DIM - 1)
            half = B_ROT_DIM // 2
            partner = jnp.where(lane < half,
                                pltpu.roll(out, tn - half, 1),
                                pltpu.roll(out, half, 1))
            out = out * cos + partner * sin

        chunk = B_QBLOCK * dilation
        if dilation == 1:
            for c in range(tm // chunk):
                o_ref[c, 0] = out[c * chunk:(c + 1) * chunk].astype(o_ref.dtype)
            return
        (buf,) = perm_scratch
        for t in range(tn // LANES):
            cols = slice(t * LANES, (t + 1) * LANES)
            buf[t] = out[:, cols]
            if chunk <= tm:
                for c in range(tm // chunk):
                    for r in range(dilation):
                        rows = pl.ds(c * chunk + r, B_QBLOCK, stride=dilation)
                        o_ref[c, r, :, cols] = buf[t, rows, :].astype(o_ref.dtype)
            else:
                for r in range(dilation):
                    rows = pl.ds(r, tm // dilation, stride=dilation)
                    o_ref[r, :, cols] = buf[t, rows, :].astype(o_ref.dtype)

    is_v = (j * tn) // B_WIDTH == 2
    pl.when(jnp.logical_not(is_v))(lambda: block(True))
    pl.when(is_v)(lambda: block(False))


def _qkv_proj(x, w, group, posb, invf, dilation, *, tm=1024, tn=1024):
    m, k = x.shape
    n = 3 * B_WIDTH
    first = group * n // tn
    chunk = B_QBLOCK * dilation
    if chunk <= tm:
        out_spec = pl.BlockSpec((tm // chunk, dilation, B_QBLOCK, tn), lambda i, j: (i, 0, 0, j))
    else:
        tiles_per_chunk = chunk // tm
        out_spec = pl.BlockSpec((None, dilation, tm // dilation, tn),
                                lambda i, j: (i // tiles_per_chunk, 0, i % tiles_per_chunk, j))
    scratch = [pltpu.VMEM((tm, B_HEAD_DIM), F32), pltpu.VMEM((tm, B_HEAD_DIM), F32)]
    if dilation > 1:
        scratch.append(pltpu.VMEM((tn // LANES, tm, LANES), F32))
    return pl.pallas_call(
        functools.partial(_qkv_kernel, tm=tm, tn=tn, dilation=dilation),
        grid=(m // tm, n // tn),
        in_specs=[pl.BlockSpec((tm, k), lambda i, j: (i, 0)),
                  pl.BlockSpec((k, tn), lambda i, j: (0, first + j)),
                  pl.BlockSpec((tm, B_HEAD_DIM), lambda i, j: (i, 0)),
                  pl.BlockSpec((1, B_HEAD_DIM), lambda i, j: (0, 0))],
        out_specs=out_spec,
        out_shape=jax.ShapeDtypeStruct((m // chunk, dilation, B_QBLOCK, n), BF16),
        scratch_shapes=scratch,
        compiler_params=_params("arbitrary", "arbitrary"),
        name=f"qkv_proj_d{dilation}",
    )(x, w, posb, invf)


def _attn_kernel(q_ref, kp_ref, kc_ref, vp_ref, vc_ref, o_ref, l_ref, *, chunks_per_seq):
    c = pl.program_id(0)
    has_prev = (c % chunks_per_seq) > 0
    q_blk = B_QBLOCK
    qi = lax.broadcasted_iota(jnp.int32, (q_blk, 2 * q_blk), 0)
    kj = lax.broadcasted_iota(jnp.int32, (q_blk, 2 * q_blk), 1)
    mask = (kj >= qi) & (kj <= qi + q_blk) & ((kj >= q_blk) | has_prev)
    scale = B_HEAD_DIM ** -0.5
    head_lane = lax.broadcasted_iota(jnp.int32, (q_blk, LANES), 1)
    lse_tile = jnp.zeros((q_blk, LANES), F32)
    for h in range(B_HEADS):
        sl = slice(h * B_HEAD_DIM, (h + 1) * B_HEAD_DIM)
        q = q_ref[:, sl]
        k2 = jnp.concatenate([kp_ref[:, sl], kc_ref[:, sl]], axis=0)
        v2 = jnp.concatenate([vp_ref[:, sl], vc_ref[:, sl]], axis=0)
        s = lax.dot_general(q, k2, (((1,), (1,)), ((), ())), preferred_element_type=F32) * scale
        s = jnp.where(mask, s, -jnp.inf)
        mx = jnp.max(s, axis=-1, keepdims=True)
        p = jnp.exp(s - mx)
        den = jnp.sum(p, axis=-1, keepdims=True)
        o = jnp.dot(p.astype(BF16), v2, preferred_element_type=F32) / den
        o_ref[:, sl] = o.astype(o_ref.dtype)
        lse_tile = jnp.where(head_lane == h, mx + jnp.log(den), lse_tile)
    l_ref[...] = lse_tile


def _dilated_attention(qkv, seq, dilation):
    n_chunks = qkv.shape[0]
    chunks_per_seq = seq // (B_QBLOCK * dilation)

    def spec(which, prev):
        def index(c, r):
            return (jnp.maximum(c - 1, 0) if prev else c, r, 0, which)
        return pl.BlockSpec((None, None, B_QBLOCK, B_WIDTH), index)

    return pl.pallas_call(
        functools.partial(_attn_kernel, chunks_per_seq=chunks_per_seq),
        grid=(n_chunks, dilation),
        in_specs=[spec(0, False), spec(1, True), spec(1, False), spec(2, True), spec(2, False)],
        out_specs=[pl.BlockSpec((None, None, B_QBLOCK, B_WIDTH), lambda c, r: (c, r, 0, 0)),
                   pl.BlockSpec((None, None, B_QBLOCK, LANES), lambda c, r: (c, r, 0, 0))],
        out_shape=[jax.ShapeDtypeStruct((n_chunks, dilation, B_QBLOCK, B_WIDTH), BF16),
                   jax.ShapeDtypeStruct((n_chunks, dilation, B_QBLOCK, LANES), F32)],
        compiler_params=_params("parallel", "parallel"),
        name=f"dilated_attn_d{dilation}",
    )(qkv, qkv, qkv, qkv, qkv)


def _combine_kernel(*refs, tm, dilations):
    n = len(dilations)
    o_refs, l_refs, out_ref = refs[:n], refs[n:2 * n], refs[2 * n]
    o_bufs, l_bufs = refs[2 * n + 1:3 * n + 1], refs[3 * n + 1:4 * n + 1]

    def scatter(src_o, src_l, o_buf, l_buf, rows):
        l_buf[rows, :] = src_l
        for h in range(B_HEADS):
            o_buf[h, rows, :] = src_o[:, h * B_HEAD_DIM:(h + 1) * B_HEAD_DIM].astype(F32)

    for o_ref, l_ref, o_buf, l_buf, d in zip(o_refs, l_refs, o_bufs, l_bufs, dilations):
        chunk = B_QBLOCK * d
        if chunk <= tm:
            for c in range(tm // chunk):
                for r in range(d):
                    rows = pl.ds(c * chunk + r, B_QBLOCK, stride=d) if d > 1 else pl.ds(c * chunk, B_QBLOCK)
                    scatter(o_ref[c, r], l_ref[c, r], o_buf, l_buf, rows)
        else:
            for r in range(d):
                scatter(o_ref[r], l_ref[r], o_buf, l_buf, pl.ds(r, tm // d, stride=d))

    lses = [l_buf[...] for l_buf in l_bufs]
    mx = functools.reduce(jnp.maximum, lses)
    es = [jnp.exp(l - mx) for l in lses]
    tot = functools.reduce(lambda a, b: a + b, es)
    ws = [e / tot for e in es]
    for h in range(B_HEADS):
        acc = None
        for w, o_buf in zip(ws, o_bufs):
            term = jnp.broadcast_to(w[:, h:h + 1], (tm, B_HEAD_DIM)) * o_buf[h]
            acc = term if acc is None else acc + term
        out_ref[:, h * B_HEAD_DIM:(h + 1) * B_HEAD_DIM] = acc.astype(out_ref.dtype)


def _combine_groups(outs, lses, dilations, m, *, tm=512):
    def spec(d, width):
        chunk = B_QBLOCK * d
        if chunk <= tm:
            return pl.BlockSpec((tm // chunk, d, B_QBLOCK, width), lambda i: (i, 0, 0, 0))
        tiles_per_chunk = chunk // tm
        return pl.BlockSpec((None, d, tm // d, width),
                            lambda i: (i // tiles_per_chunk, 0, i % tiles_per_chunk, 0))

    n = len(dilations)
    return pl.pallas_call(
        functools.partial(_combine_kernel, tm=tm, dilations=dilations),
        grid=(m // tm,),
        in_specs=[spec(d, B_WIDTH) for d in dilations] + [spec(d, LANES) for d in dilations],
        out_specs=pl.BlockSpec((tm, B_WIDTH), lambda i: (i, 0)),
        out_shape=jax.ShapeDtypeStruct((m, B_WIDTH), BF16),
        scratch_shapes=[pltpu.VMEM((B_HEADS, tm, B_HEAD_DIM), F32)] * n + [pltpu.VMEM((tm, LANES), F32)] * n,
        compiler_params=_params("parallel"),
        name="attn_combine",
    )(*outs, *lses)


def _gate_kernel(u_ref, v_ref, g_ref, b_ref, ws_ref, bs_ref, o_ref, *, rows):
    vn = _layer_norm(v_ref[...].astype(F32), g_ref[...], b_ref[...]).astype(BF16)
    ti = lax.broadcasted_iota(jnp.int32, (A_CHUNK, A_CHUNK), 0)
    si = lax.broadcasted_iota(jnp.int32, (A_CHUNK, A_CHUNK), 1)
    causal = si <= ti
    for g in range(A_GROUPS):
        cols = slice(g * A_GROUP_DIM, (g + 1) * A_GROUP_DIM)
        w = jnp.where(causal, ws_ref[g], 0.0).astype(BF16)
        bias = bs_ref[:, cols]
        for c in range(rows // A_CHUNK):
            rws = slice(c * A_CHUNK, (c + 1) * A_CHUNK)
            mixed = jnp.dot(w, vn[rws, cols], preferred_element_type=F32) + bias
            o_ref[rws, cols] = (u_ref[rws, cols].astype(F32) * mixed).astype(o_ref.dtype)


def _spatial_gate(h, ln_g, ln_b, w_s, bs_exp, *, rows=256):
    m = h.shape[0]
    width = D_MODEL
    return pl.pallas_call(
        functools.partial(_gate_kernel, rows=rows),
        grid=(m // rows,),
        in_specs=[pl.BlockSpec((rows, width), lambda i: (i, 0)),
                  pl.BlockSpec((rows, width), lambda i: (i, 1)),
                  pl.BlockSpec((1, width), lambda i: (0, 0)),
                  pl.BlockSpec((1, width), lambda i: (0, 0)),
                  pl.BlockSpec((A_GROUPS, A_CHUNK, A_CHUNK), lambda i: (0, 0, 0)),
                  pl.BlockSpec((A_CHUNK, width), lambda i: (0, 0))],
        out_specs=pl.BlockSpec((rows, width), lambda i: (i, 0)),
        out_shape=jax.ShapeDtypeStruct((m, width), BF16),
        compiler_params=_params("parallel"),
        name="spatial_gate",
    )(h, h, ln_g, ln_b, w_s, bs_exp)


def _pool_kernel(h_ref, halo_ref, wg_ref, scale_ref, o_ref, *, rows, seq):
    i = pl.program_id(0)
    t0 = (i * rows) % seq
    pos = t0 + lax.broadcasted_iota(jnp.int32, (rows, 1), 0)
    keep_halo = t0 > 0
    for g, window in enumerate(C_WINDOWS):
        cols = slice(g * C_GROUP_DIM, (g + 1) * C_GROUP_DIM)
        hg = h_ref[:, cols]
        halo = jnp.where(keep_halo, halo_ref[:, cols], 0.0)
        acc = jnp.concatenate([halo, hg], axis=0)
        span = 1
        while span < window:
            acc = acc + pltpu.roll(acc, span, 0)
            span *= 2
        den = jnp.minimum(pos + 1, window).astype(F32)
        pooled = acc[C_HALO:, :] / den - hg
        y = jnp.dot(pooled.astype(BF16), wg_ref[g], preferred_element_type=F32) * scale_ref[:, cols]
        o_ref[:, cols] = y.astype(o_ref.dtype)


def _pool_mix(h, w_group, scale, seq, *, rows=256):
    m, width = h.shape
    halo_per_block = rows // C_HALO
    return pl.pallas_call(
        functools.partial(_pool_kernel, rows=rows, seq=seq),
        grid=(m // rows,),
        in_specs=[pl.BlockSpec((rows, width), lambda i: (i, 0)),
                  pl.BlockSpec((C_HALO, width), lambda i: (jnp.maximum(i * halo_per_block - 1, 0), 0)),
                  pl.BlockSpec((len(C_WINDOWS), C_GROUP_DIM, C_GROUP_DIM), lambda i: (0, 0, 0)),
                  pl.BlockSpec((1, width), lambda i: (0, 0))],
        out_specs=pl.BlockSpec((rows, width), lambda i: (i, 0)),
        out_shape=jax.ShapeDtypeStruct((m, width), BF16),
        compiler_params=_params("parallel"),
        name="pool_mix",
    )(h, h, w_group, scale)


def _ffn_up_kernel(*refs, tm, nj, tail_width, tiles_per_seq, n_jobs):
    x_ref, wg_ref, wu_ref, cwg_ref, cwu_ref, cbg_ref, cbu_ref = refs[:7]
    o_ref = refs[7 + n_jobs]
    hbuf, carry = refs[-2:]
    i = pl.program_id(0)
    j = pl.program_id(1)
    seq_start = (i % tiles_per_seq) == 0

    @pl.when((i == 0) & (j == 0))
    def _():
        carry[...] = jnp.zeros(carry.shape, F32)

    @pl.when(seq_start)
    def _():
        hbuf[:, 0:CARRY_ROWS, :] = jnp.zeros((2, CARRY_ROWS, FF_BLOCK), F32)

    @pl.when(jnp.logical_not(seq_start))
    def _():
        hbuf[:, 0:CARRY_ROWS, :] = carry[j]

    _run_side_casts(refs[7:7 + n_jobs], refs[8 + n_jobs:-2])

    def block(width):
        x = x_ref[...]
        halves = []
        for s, (w_ref, cw_ref, cb_ref) in enumerate(((wg_ref, cwg_ref, cbg_ref), (wu_ref, cwu_ref, cbu_ref))):
            h = jnp.dot(x, w_ref[:, :width], preferred_element_type=F32)
            hbuf[s, CARRY_ROWS:, :width] = h
            carry[j, s, :, :width] = h[tm - CARRY_ROWS:, :]
            h1 = hbuf[s, CARRY_ROWS - 1:CARRY_ROWS - 1 + tm, :width]
            h2 = hbuf[s, CARRY_ROWS - 2:CARRY_ROWS - 2 + tm, :width]
            halves.append(cw_ref[2:3, :width] * h + cw_ref[1:2, :width] * h1 + cw_ref[0:1, :width] * h2
                          + cb_ref[:, :width])
        gate, up = halves
        o_ref[:, :width] = (gate / (1.0 + jnp.exp(-gate)) * up).astype(o_ref.dtype)

    if tail_width == FF_BLOCK:
        block(FF_BLOCK)
    else:
        pl.when(j < nj - 1)(lambda: block(FF_BLOCK))
        pl.when(j == nj - 1)(lambda: block(tail_width))


def _ffn_up(x, w_gate, w_up, cw_gate, cw_up, cb_gate, cb_up, seq, *, cast_jobs=(), tm=1024):
    m, k = x.shape
    d_ff = w_gate.shape[1]
    nj = pl.cdiv(d_ff, FF_BLOCK)
    col = lambda i, j: (0, j)
    c_in, c_out, c_shapes = _side_cast_plumbing(cast_jobs, (m // tm) * nj, lambda i, j: i * nj + j)
    res = pl.pallas_call(
        functools.partial(_ffn_up_kernel, tm=tm, nj=nj, tail_width=d_ff - (nj - 1) * FF_BLOCK,
                          tiles_per_seq=seq // tm, n_jobs=len(cast_jobs)),
        grid=(m // tm, nj),
        in_specs=[pl.BlockSpec((tm, k), lambda i, j: (i, 0), pipeline_mode=pl.Buffered(1)),
                  pl.BlockSpec((k, FF_BLOCK), col), pl.BlockSpec((k, FF_BLOCK), col),
                  pl.BlockSpec((CONV_WIDTH, FF_BLOCK), col), pl.BlockSpec((CONV_WIDTH, FF_BLOCK), col),
                  pl.BlockSpec((1, FF_BLOCK), col), pl.BlockSpec((1, FF_BLOCK), col)] + c_in,
        out_specs=[pl.BlockSpec((tm, FF_BLOCK), lambda i, j: (i, j))] + c_out,
        out_shape=[jax.ShapeDtypeStruct((m, d_ff), BF16)] + c_shapes,
        scratch_shapes=[pltpu.VMEM((2, tm + CARRY_ROWS, FF_BLOCK), F32),
                        pltpu.VMEM((nj, 2, CARRY_ROWS, FF_BLOCK), F32)],
        compiler_params=_params("arbitrary", "arbitrary"),
        name="ffn_up_conv_gate",
    )(x, w_gate, w_up, cw_gate, cw_up, cb_gate, cb_up, *[job[0] for job in cast_jobs])
    return res[0], res[1:]


def _proj_ln_kernel(*refs, nk, k_tail, n_chunks, n_jobs):
    a_ref, w_ref, x_ref, g_ref, b_ref = refs[:5]
    of_ref, ob_ref = refs[5 + n_jobs:7 + n_jobs]
    mu_ref, rstd_ref = refs[-2:]
    _run_side_casts(refs[5:5 + n_jobs], refs[7 + n_jobs:-2])
    k = pl.program_id(1)
    tm, n = of_ref.shape
    cw = n // n_chunks

    def add_residual(c):
        of_ref[:, c * cw:(c + 1) * cw] += DEEPNORM_ALPHA * x_ref[...]

    @pl.when(k == 0)
    def _():
        of_ref[...] = jnp.dot(a_ref[...], w_ref[...], preferred_element_type=F32)
        add_residual(0)

    for c in range(1, n_chunks):
        pl.when(k == c)(functools.partial(add_residual, c))

    @pl.when((k > 0) & (k < nk - 1))
    def _():
        of_ref[...] += jnp.dot(a_ref[...], w_ref[...], preferred_element_type=F32)

    @pl.when(k == nk - 1)
    def _():
        z = of_ref[...] + jnp.dot(a_ref[:, :k_tail], w_ref[:k_tail, :], preferred_element_type=F32)
        of_ref[...] = z
        d = z - z[:, 0:1]
        s1 = jnp.mean(d, axis=-1, keepdims=True)
        var = jnp.maximum(jnp.mean(d * d, axis=-1, keepdims=True) - s1 * s1, 0.0)
        mu_ref[...] = jnp.broadcast_to(z[:, 0:1] + s1, (tm, LANES))
        rstd_ref[...] = jnp.broadcast_to(lax.rsqrt(var + LN_EPS), (tm, LANES))

        def rows_at(r):
            return pl.ds(pl.multiple_of(r * LN_ROWS, LN_ROWS), LN_ROWS)

        def across(stat_ref, r):
            return jnp.concatenate([stat_ref[rows_at(r), :]] * (n // LANES), axis=1)

        def norm_pass(r, carry):
            d = of_ref[rows_at(r), :] - across(mu_ref, r)
            y = d * across(rstd_ref, r) * g_ref[...] + b_ref[...]
            of_ref[rows_at(r), :] = y
            ob_ref[rows_at(r), :] = y.astype(ob_ref.dtype)
            return carry

        lax.fori_loop(0, tm // LN_ROWS, norm_pass, 0, unroll=4)


def _proj_ln(a, w, x, g, b, *, name, cast_jobs=(), tm=512, tk=1024):
    m, kdim = a.shape
    n = w.shape[1]
    nk = pl.cdiv(kdim, tk)
    k_tail = kdim - (nk - 1) * tk
    n_chunks = 1
    while n_chunks * 2 <= min(nk, 8):
        n_chunks *= 2
    cw = n // n_chunks
    c_in, c_out, c_shapes = _side_cast_plumbing(cast_jobs, (m // tm) * nk, lambda i, k: i * nk + k)
    res = pl.pallas_call(
        functools.partial(_proj_ln_kernel, nk=nk, k_tail=k_tail, n_chunks=n_chunks, n_jobs=len(cast_jobs)),
        grid=(m // tm, nk),
        in_specs=[pl.BlockSpec((tm, tk), lambda i, k: (i, k)),
                  pl.BlockSpec((tk, n), lambda i, k: (k, 0)),
                  pl.BlockSpec((tm, cw), lambda i, k: (i, jnp.minimum(k, n_chunks - 1))),
                  pl.BlockSpec((LN_ROWS, n), lambda i, k: (0, 0)),
                  pl.BlockSpec((LN_ROWS, n), lambda i, k: (0, 0))] + c_in,
        out_specs=[pl.BlockSpec((tm, n), lambda i, k: (i, 0)),
                   pl.BlockSpec((tm, n), lambda i, k: (i, 0))] + c_out,
        out_shape=[jax.ShapeDtypeStruct((m, n), F32), jax.ShapeDtypeStruct((m, n), BF16)] + c_shapes,
        scratch_shapes=[pltpu.VMEM((tm, LANES), F32), pltpu.VMEM((tm, LANES), F32)],
        compiler_params=_params("arbitrary", "arbitrary"),
        name=name,
    )(a, w, x, jnp.broadcast_to(g.reshape(1, n), (LN_ROWS, n)), jnp.broadcast_to(b.reshape(1, n), (LN_ROWS, n)),
      *[job[0] for job in cast_jobs])
    return res[0], res[1], res[2:]


def _rotary_inv_freq():
    half = B_ROT_DIM // 2
    inv_freq = jnp.float32(ROPE_THETA) ** (-jnp.arange(half, dtype=F32) * 2.0 / B_ROT_DIM)
    zeros = jnp.zeros((B_HEAD_DIM - B_ROT_DIM,), F32)
    return jnp.concatenate([-inv_freq, inv_freq, zeros]).reshape(1, B_HEAD_DIM)


def kernel(x, positions, a_w_in, a_ln_g, a_ln_b, a_w_s, a_b_s, a_w_out, b_w_in, b_w_out, c_w_in, c_w_group, c_scale, c_w_out, f_w_up, f_conv_w, f_conv_b, f_w_down, ln_mix_g, ln_mix_b, ln_ffn_g, ln_ffn_b):
    batch, seq, d = x.shape
    m = batch * seq
    xf = x.reshape(m, d)
    xb = xf.astype(BF16)
    posb = jnp.broadcast_to(positions.reshape(m, 1).astype(F32), (m, B_HEAD_DIM))
    invf = _rotary_inv_freq()

    c_w_group_2d = c_w_group.reshape(c_w_group.shape[0], -1, c_w_group.shape[-1])
    a_in = [a_w_in[0].astype(BF16), None]
    a_out = [None, None]
    ffn_up_w = None
    for layer in range(DEPTH):
        kind, idx = layer % N_MIXERS, layer // N_MIXERS
        if kind == 0:
            jobs = [(f_w_up, 0, True), (a_w_out, 0, False)] if layer == 0 else []
            h, casts = _mm(xb, a_in[idx], epilogue=_gelu, out_dtype=BF16, name="gmlp_in_gelu", cast_jobs=jobs)
            if layer == 0:
                ffn_up_w, a_out[0] = casts[:2], casts[2]
            bs_exp = jnp.repeat(a_b_s[idx].T, A_GROUP_DIM, axis=1)
            y = _spatial_gate(h, a_ln_g[idx].reshape(1, d), a_ln_b[idx].reshape(1, d), a_w_s[idx], bs_exp)
            w_out = a_out[idx]
        elif kind == 1:
            outs, lses, dilations = [], [], []
            for group, (window, dilation) in enumerate(B_PATTERNS):
                assert window // dilation == B_QBLOCK
                qkv = _qkv_proj(xb, b_in, group, posb, invf, dilation)
                o, lse = _dilated_attention(qkv, seq, dilation)
                outs.append(o)
                lses.append(lse)
                dilations.append(dilation)
            y = _combine_groups(outs, lses, tuple(dilations), m)
            w_out = b_out
        else:
            h, _ = _mm(xb, c_in, epilogue=lambda t: t, out_dtype=F32, name="pool_in")
            y = _pool_mix(h, c_group.reshape(c_w_group[idx].shape), c_scale[idx].reshape(1, d), seq)
            w_out = c_out
        xf, xb, _ = _proj_ln(y, w_out, xf, ln_mix_g[layer], ln_mix_b[layer], name="mixer_out_ln")

        jobs = [(f_w_down, layer, False)]
        nxt = layer + 1
        nxt_kind, nxt_idx = nxt % N_MIXERS, nxt // N_MIXERS
        if nxt < DEPTH and nxt_kind == 0:
            jobs += [(a_w_in, nxt_idx, False), (a_w_out, nxt_idx, False)]
        elif nxt < DEPTH and nxt_kind == 1:
            jobs += [(b_w_in, nxt_idx, False), (b_w_out, nxt_idx, False)]
        elif nxt < DEPTH:
            jobs += [(c_w_in, nxt_idx, False), (c_w_out, nxt_idx, False), (c_w_group_2d, nxt_idx, False)]
        cw, cb = f_conv_w[layer], f_conv_b[layer].reshape(1, -1)
        g, casts = _ffn_up(xb, ffn_up_w[0], ffn_up_w[1], cw[:, :D_FF], cw[:, D_FF:], cb[:, :D_FF], cb[:, D_FF:],
                           seq, cast_jobs=jobs)
        w_down = casts[0]
        if nxt < DEPTH and nxt_kind == 0:
            a_in[nxt_idx], a_out[nxt_idx] = casts[1:3]
        elif nxt < DEPTH and nxt_kind == 1:
            b_in, b_out = casts[1:3]
        elif nxt < DEPTH:
            c_in, c_out, c_group = casts[1:4]
        jobs = [(f_w_up, nxt, True)] if nxt < DEPTH else []
        xf, xb, casts = _proj_ln(g, w_down, xf, ln_ffn_g[layer], ln_ffn_b[layer], name="ffn_down_ln",
                                 cast_jobs=jobs)
        if nxt < DEPTH:
            ffn_up_w = casts

    return xf.reshape(batch, seq, d)
```

```python
import functools
import math

import jax
import jax.numpy as jnp
from jax import lax
from jax.experimental import pallas as pl
from jax.experimental.pallas import tpu as pltpu

F32 = jnp.float32
BF16 = jnp.bfloat16

D_MODEL = 4096
DEPTH = 4
N_MIXERS = 3

A_CHUNK = 128
A_GROUPS = 16
A_GROUP_DIM = D_MODEL // A_GROUPS

B_PATTERNS = ((128, 1), (512, 4), (2048, 16))
B_HEADS = 16
B_HEAD_DIM = 128
B_ROT_DIM = B_HEAD_DIM // 4
B_QBLOCK = 128
B_WIDTH = B_HEADS * B_HEAD_DIM
ROPE_THETA = 500000.0

C_WINDOWS = (2, 4, 8, 16)
C_GROUP_DIM = D_MODEL // len(C_WINDOWS)
C_HALO = 16

D_FF = 11008
FF_BLOCK = 768
CONV_WIDTH = 3
CARRY_ROWS = 8

LN_EPS = 1e-5
LN_ROWS = 8
DEEPNORM_ALPHA = (2 * DEPTH) ** 0.25

LANES = 128
SIDE_CAST_WINDOW_BUDGET_BYTES = 12 * 1024 * 1024
V7X_VMEM_LIMIT_BYTES = 63 * 1024 * 1024


def _params(*semantics, flags=None):
    return pltpu.CompilerParams(dimension_semantics=semantics, vmem_limit_bytes=V7X_VMEM_LIMIT_BYTES,
                                flags=flags)


def _layer_norm(z, g, b):
    mu = jnp.mean(z, axis=-1, keepdims=True)
    d = z - mu
    var = jnp.mean(d * d, axis=-1, keepdims=True)
    return d * lax.rsqrt(var + LN_EPS) * g + b


def _cast_slab_rows(total_rows, n_steps, smallest=16):
    rows = smallest
    while total_rows % rows or total_rows // rows > n_steps:
        rows *= 2
        assert rows <= total_rows
    return rows


def _side_cast_plumbing(jobs, n_steps, step_of):
    in_specs, out_specs, out_shapes = [], [], []
    for src, layer, split in jobs:
        _, rows, width = src.shape
        slab = _cast_slab_rows(rows, n_steps)
        slab_of = lambda *g, last=rows // slab - 1: jnp.minimum(step_of(*g), last)
        in_specs.append(pl.BlockSpec((None, slab, width), lambda *g, f=slab_of, l=layer: (l, f(*g), 0)))
        parts = 2 if split else 1
        out_specs += [pl.BlockSpec((slab, width // parts), lambda *g, f=slab_of: (f(*g), 0))] * parts
        out_shapes += [jax.ShapeDtypeStruct((rows, width // parts), BF16)] * parts
    return in_specs, out_specs, out_shapes


def _row_tile_buffers(jobs, n_steps):
    window_bytes = 0
    for src, _, _ in jobs:
        _, rows, width = src.shape
        window_bytes += 2 * _cast_slab_rows(rows, n_steps) * width * (4 + 2)
    return 1 if window_bytes > SIDE_CAST_WINDOW_BUDGET_BYTES else 2


def _run_side_casts(src_refs, dst_refs):
    dst = list(dst_refs)
    for src in src_refs:
        width = dst[0].shape[1]
        for part in range(src.shape[1] // width):
            dst.pop(0)[...] = src[:, part * width:(part + 1) * width].astype(BF16)
    assert not dst


def _gelu(x):
    return 0.5 * x * (1.0 + lax.erf(x * (1.0 / math.sqrt(2.0))))


def _mm_kernel(*refs, epilogue, n_jobs):
    x_ref, w_ref = refs[:2]
    o_ref = refs[2 + n_jobs]
    _run_side_casts(refs[2:2 + n_jobs], refs[3 + n_jobs:])
    acc = jnp.dot(x_ref[...], w_ref[...], preferred_element_type=F32)
    o_ref[...] = epilogue(acc).astype(o_ref.dtype)


def _mm(x, w, *, epilogue, out_dtype, name, cast_jobs=(), tm=1024, tn=1024):
    m, k = x.shape
    n = w.shape[1]
    nj = n // tn
    c_in, c_out, c_shapes = _side_cast_plumbing(cast_jobs, (m // tm) * nj, lambda i, j: i * nj + j)
    res = pl.pallas_call(
        functools.partial(_mm_kernel, epilogue=epilogue, n_jobs=len(cast_jobs)),
        grid=(m // tm, nj),
        in_specs=[pl.BlockSpec((tm, k), lambda i, j: (i, 0),
                               pipeline_mode=pl.Buffered(_row_tile_buffers(cast_jobs, (m // tm) * nj))),
                  pl.BlockSpec((k, tn), lambda i, j: (0, j))] + c_in,
        out_specs=[pl.BlockSpec((tm, tn), lambda i, j: (i, j))] + c_out,
        out_shape=[jax.ShapeDtypeStruct((m, n), out_dtype)] + c_shapes,
        compiler_params=_params("arbitrary", "arbitrary"),
        name=name,
    )(x, w, *[job[0] for job in cast_jobs])
    return res[0], res[1:]


def _qkv_kernel(*refs, tm, tn, dilation, n_jobs):
    x_ref, w_ref, pos_ref, invf_ref = refs[:4]
    o_ref = refs[4 + n_jobs]
    n_dst = len(refs) - (7 + n_jobs) - (dilation > 1)
    cos_ref, sin_ref, *perm_scratch = refs[5 + n_jobs + n_dst:]
    _run_side_casts(refs[4:4 + n_jobs], refs[5 + n_jobs:5 + n_jobs + n_dst])
    j = pl.program_id(1)

    @pl.when(j == 0)
    def _():
        ang = pos_ref[...] * invf_ref[...]
        cos_ref[...] = jnp.cos(ang)
        sin_ref[...] = jnp.sin(ang)

    def block(rotate):
        out = jnp.dot(x_ref[...], w_ref[...], preferred_element_type=F32)
        if rotate:
            reps = tn // B_HEAD_DIM
            cos = jnp.concatenate([cos_ref[...]] * reps, axis=1)
            sin = jnp.concatenate([sin_ref[...]] * reps, axis=1)
            lane = lax.broadcasted_iota(jnp.int32, out.shape, 1) & (B_HEAD_DIM - 1)
            half = B_ROT_DIM // 2
            partner = jnp.where(lane < half,
                                pltpu.roll(out, tn - half, 1),
                                pltpu.roll(out, half, 1))
            out = out * cos + partner * sin

        chunk = B_QBLOCK * dilation
        if dilation == 1:
            for c in range(tm // chunk):
                o_ref[c, 0] = out[c * chunk:(c + 1) * chunk].astype(o_ref.dtype)
            return
        (buf,) = perm_scratch
        for t in range(tn // LANES):
            cols = slice(t * LANES, (t + 1) * LANES)
            buf[t] = out[:, cols]
            if chunk <= tm:
                for c in range(tm // chunk):
                    for r in range(dilation):
                        rows = pl.ds(c * chunk + r, B_QBLOCK, stride=dilation)
                        o_ref[c, r, :, cols] = buf[t, rows, :].astype(o_ref.dtype)
            else:
                for r in range(dilation):
                    rows = pl.ds(r, tm // dilation, stride=dilation)
                    o_ref[r, :, cols] = buf[t, rows, :].astype(o_ref.dtype)

    is_v = (j * tn) // B_WIDTH == 2
    pl.when(jnp.logical_not(is_v))(lambda: block(True))
    pl.when(is_v)(lambda: block(False))


def _qkv_proj(x, w, group, posb, invf, dilation, *, cast_jobs=(), tm=1024, tn=1024):
    m, k = x.shape
    n = 3 * B_WIDTH
    nj = n // tn
    first = group * nj
    c_in, c_out, c_shapes = _side_cast_plumbing(cast_jobs, (m // tm) * nj, lambda i, j: i * nj + j)
    chunk = B_QBLOCK * dilation
    if chunk <= tm:
        out_spec = pl.BlockSpec((tm // chunk, dilation, B_QBLOCK, tn), lambda i, j: (i, 0, 0, j))
    else:
        tiles_per_chunk = chunk // tm
        out_spec = pl.BlockSpec((None, dilation, tm // dilation, tn),
                                lambda i, j: (i // tiles_per_chunk, 0, i % tiles_per_chunk, j))
    scratch = [pltpu.VMEM((tm, B_HEAD_DIM), F32), pltpu.VMEM((tm, B_HEAD_DIM), F32)]
    if dilation > 1:
        scratch.append(pltpu.VMEM((tn // LANES, tm, LANES), F32))
    res = pl.pallas_call(
        functools.partial(_qkv_kernel, tm=tm, tn=tn, dilation=dilation, n_jobs=len(cast_jobs)),
        grid=(m // tm, nj),
        in_specs=[pl.BlockSpec((tm, k), lambda i, j: (i, 0),
                               pipeline_mode=pl.Buffered(_row_tile_buffers(cast_jobs, (m // tm) * nj))),
                  pl.BlockSpec((k, tn), lambda i, j: (0, first + j)),
                  pl.BlockSpec((tm, B_HEAD_DIM), lambda i, j: (i, 0)),
                  pl.BlockSpec((1, B_HEAD_DIM), lambda i, j: (0, 0))] + c_in,
        out_specs=[out_spec] + c_out,
        out_shape=[jax.ShapeDtypeStruct((m // chunk, dilation, B_QBLOCK, n), BF16)] + c_shapes,
        scratch_shapes=scratch,
        compiler_params=_params("arbitrary", "arbitrary"),
        name=f"qkv_proj_d{dilation}",
    )(x, w, posb, invf, *[job[0] for job in cast_jobs])
    return res[0], res[1:]


def _attn_kernel(q_ref, kp_ref, kc_ref, vp_ref, vc_ref, o_ref, l_ref, *, chunks_per_seq):
    c = pl.program_id(0)
    has_prev = (c % chunks_per_seq) > 0
    q_blk = B_QBLOCK
    qi = lax.broadcasted_iota(jnp.int32, (q_blk, 2 * q_blk), 0)
    kj = lax.broadcasted_iota(jnp.int32, (q_blk, 2 * q_blk), 1)
    mask = (kj >= qi) & (kj <= qi + q_blk) & ((kj >= q_blk) | has_prev)
    scale = B_HEAD_DIM ** -0.5
    head_lane = lax.broadcasted_iota(jnp.int32, (q_blk, LANES), 1)
    lse_tile = jnp.zeros((q_blk, LANES), F32)
    for h in range(B_HEADS):
        sl = slice(h * B_HEAD_DIM, (h + 1) * B_HEAD_DIM)
        q = q_ref[:, sl]
        k2 = jnp.concatenate([kp_ref[:, sl], kc_ref[:, sl]], axis=0)
        v2 = jnp.concatenate([vp_ref[:, sl], vc_ref[:, sl]], axis=0)
        s = lax.dot_general(q, k2, (((1,), (1,)), ((), ())), preferred_element_type=F32) * scale
        s = jnp.where(mask, s, -jnp.inf)
        mx = jnp.max(s, axis=-1, keepdims=True)
        p = jnp.exp(s - mx)
        den = jnp.sum(p, axis=-1, keepdims=True)
        o = jnp.dot(p.astype(BF16), v2, preferred_element_type=F32) / den
        o_ref[:, sl] = o.astype(o_ref.dtype)
        lse_tile = jnp.where(head_lane == h, mx + jnp.log(den), lse_tile)
    l_ref[...] = lse_tile


def _dilated_attention(qkv, seq, dilation):
    n_chunks = qkv.shape[0]
    chunks_per_seq = seq // (B_QBLOCK * dilation)

    def spec(which, prev):
        def index(c, r):
            return (jnp.maximum(c - 1, 0) if prev else c, r, 0, which)
        return pl.BlockSpec((None, None, B_QBLOCK, B_WIDTH), index)

    return pl.pallas_call(
        functools.partial(_attn_kernel, chunks_per_seq=chunks_per_seq),
        grid=(n_chunks, dilation),
        in_specs=[spec(0, False), spec(1, True), spec(1, False), spec(2, True), spec(2, False)],
        out_specs=[pl.BlockSpec((None, None, B_QBLOCK, B_WIDTH), lambda c, r: (c, r, 0, 0)),
                   pl.BlockSpec((None, None, B_QBLOCK, LANES), lambda c, r: (c, r, 0, 0))],
        out_shape=[jax.ShapeDtypeStruct((n_chunks, dilation, B_QBLOCK, B_WIDTH), BF16),
                   jax.ShapeDtypeStruct((n_chunks, dilation, B_QBLOCK, LANES), F32)],
        compiler_params=_params("parallel", "parallel"),
        name=f"dilated_attn_d{dilation}",
    )(qkv, qkv, qkv, qkv, qkv)


def _combine_kernel(*refs, tm, dilations):
    n = len(dilations)
    o_refs, l_refs, out_ref = refs[:n], refs[n:2 * n], refs[2 * n]
    o_bufs, l_bufs = refs[2 * n + 1:3 * n + 1], refs[3 * n + 1:4 * n + 1]

    def scatter(src_o, src_l, o_buf, l_buf, rows):
        l_buf[rows, :] = src_l
        for h in range(B_HEADS):
            o_buf[h, rows, :] = src_o[:, h * B_HEAD_DIM:(h + 1) * B_HEAD_DIM].astype(F32)

    for o_ref, l_ref, o_buf, l_buf, d in zip(o_refs, l_refs, o_bufs, l_bufs, dilations):
        chunk = B_QBLOCK * d
        if chunk <= tm:
            for c in range(tm // chunk):
                for r in range(d):
                    rows = pl.ds(c * chunk + r, B_QBLOCK, stride=d) if d > 1 else pl.ds(c * chunk, B_QBLOCK)
                    scatter(o_ref[c, r], l_ref[c, r], o_buf, l_buf, rows)
        else:
            for r in range(d):
                scatter(o_ref[r], l_ref[r], o_buf, l_buf, pl.ds(r, tm // d, stride=d))

    lses = [l_buf[...] for l_buf in l_bufs]
    mx = functools.reduce(jnp.maximum, lses)
    es = [jnp.exp(l - mx) for l in lses]
    tot = functools.reduce(lambda a, b: a + b, es)
    ws = [e / tot for e in es]
    for h in range(B_HEADS):
        acc = None
        for w, o_buf in zip(ws, o_bufs):
            term = jnp.broadcast_to(w[:, h:h + 1], (tm, B_HEAD_DIM)) * o_buf[h]
            acc = term if acc is None else acc + term
        out_ref[:, h * B_HEAD_DIM:(h + 1) * B_HEAD_DIM] = acc.astype(out_ref.dtype)


def _combine_groups(outs, lses, dilations, m, *, tm=512):
    def spec(d, width):
        chunk = B_QBLOCK * d
        if chunk <= tm:
            return pl.BlockSpec((tm // chunk, d, B_QBLOCK, width), lambda i: (i, 0, 0, 0))
        tiles_per_chunk = chunk // tm
        return pl.BlockSpec((None, d, tm // d, width),
                            lambda i: (i // tiles_per_chunk, 0, i % tiles_per_chunk, 0))

    n = len(dilations)
    return pl.pallas_call(
        functools.partial(_combine_kernel, tm=tm, dilations=dilations),
        grid=(m // tm,),
        in_specs=[spec(d, B_WIDTH) for d in dilations] + [spec(d, LANES) for d in dilations],
        out_specs=pl.BlockSpec((tm, B_WIDTH), lambda i: (i, 0)),
        out_shape=jax.ShapeDtypeStruct((m, B_WIDTH), BF16),
        scratch_shapes=[pltpu.VMEM((B_HEADS, tm, B_HEAD_DIM), F32)] * n + [pltpu.VMEM((tm, LANES), F32)] * n,
        compiler_params=_params("parallel"),
        name="attn_combine",
    )(*outs, *lses)


def _gate_kernel(u_ref, v_ref, g_ref, b_ref, ws_ref, bs_ref, o_ref, *, rows):
    vn = _layer_norm(v_ref[...].astype(F32), g_ref[...], b_ref[...]).astype(BF16)
    ti = lax.broadcasted_iota(jnp.int32, (A_CHUNK, A_CHUNK), 0)
    si = lax.broadcasted_iota(jnp.int32, (A_CHUNK, A_CHUNK), 1)
    causal = si <= ti
    for g in range(A_GROUPS):
        cols = slice(g * A_GROUP_DIM, (g + 1) * A_GROUP_DIM)
        w = jnp.where(causal, ws_ref[g], 0.0).astype(BF16)
        bias = bs_ref[:, cols]
        for c in range(rows // A_CHUNK):
            rws = slice(c * A_CHUNK, (c + 1) * A_CHUNK)
            mixed = jnp.dot(w, vn[rws, cols], preferred_element_type=F32) + bias
            o_ref[rws, cols] = (u_ref[rws, cols].astype(F32) * mixed).astype(o_ref.dtype)


def _spatial_gate(h, ln_g, ln_b, w_s, bs_exp, *, rows=256):
    m = h.shape[0]
    width = D_MODEL
    return pl.pallas_call(
        functools.partial(_gate_kernel, rows=rows),
        grid=(m // rows,),
        in_specs=[pl.BlockSpec((rows, width), lambda i: (i, 0)),
                  pl.BlockSpec((rows, width), lambda i: (i, 1)),
                  pl.BlockSpec((1, width), lambda i: (0, 0)),
                  pl.BlockSpec((1, width), lambda i: (0, 0)),
                  pl.BlockSpec((A_GROUPS, A_CHUNK, A_CHUNK), lambda i: (0, 0, 0)),
                  pl.BlockSpec((A_CHUNK, width), lambda i: (0, 0))],
        out_specs=pl.BlockSpec((rows, width), lambda i: (i, 0)),
        out_shape=jax.ShapeDtypeStruct((m, width), BF16),
        compiler_params=_params("parallel"),
        name="spatial_gate",
    )(h, h, ln_g, ln_b, w_s, bs_exp)


def _pool_kernel(h_ref, halo_ref, wg_ref, scale_ref, o_ref, *, rows, seq):
    i = pl.program_id(0)
    t0 = (i * rows) % seq
    pos = t0 + lax.broadcasted_iota(jnp.int32, (rows, 1), 0)
    keep_halo = t0 > 0
    for g, window in enumerate(C_WINDOWS):
        cols = slice(g * C_GROUP_DIM, (g + 1) * C_GROUP_DIM)
        hg = h_ref[:, cols]
        halo = jnp.where(keep_halo, halo_ref[:, cols], 0.0)
        acc = jnp.concatenate([halo, hg], axis=0)
        span = 1
        while span < window:
            acc = acc + pltpu.roll(acc, span, 0)
            span *= 2
        den = jnp.minimum(pos + 1, window).astype(F32)
        pooled = acc[C_HALO:, :] / den - hg
        y = jnp.dot(pooled.astype(BF16), wg_ref[g], preferred_element_type=F32) * scale_ref[:, cols]
        o_ref[:, cols] = y.astype(o_ref.dtype)


def _pool_mix(h, w_group, scale, seq, *, rows=256):
    m, width = h.shape
    halo_per_block = rows // C_HALO
    return pl.pallas_call(
        functools.partial(_pool_kernel, rows=rows, seq=seq),
        grid=(m // rows,),
        in_specs=[pl.BlockSpec((rows, width), lambda i: (i, 0)),
                  pl.BlockSpec((C_HALO, width), lambda i: (jnp.maximum(i * halo_per_block - 1, 0), 0)),
                  pl.BlockSpec((len(C_WINDOWS), C_GROUP_DIM, C_GROUP_DIM), lambda i: (0, 0, 0)),
                  pl.BlockSpec((1, width), lambda i: (0, 0))],
        out_specs=pl.BlockSpec((rows, width), lambda i: (i, 0)),
        out_shape=jax.ShapeDtypeStruct((m, width), BF16),
        compiler_params=_params("parallel"),
        name="pool_mix",
    )(h, h, w_group, scale)


def _ffn_up_kernel(*refs, tm, nj, tail_width, tiles_per_seq, n_jobs):
    x_ref, wg_ref, wu_ref, cwg_ref, cwu_ref, cbg_ref, cbu_ref = refs[:7]
    o_ref = refs[7 + n_jobs]
    hbuf, carry = refs[-2:]
    i = pl.program_id(0)
    j = pl.program_id(1)
    seq_start = (i % tiles_per_seq) == 0

    @pl.when((i == 0) & (j == 0))
    def _():
        carry[...] = jnp.zeros(carry.shape, F32)

    @pl.when(seq_start)
    def _():
        hbuf[:, 0:CARRY_ROWS, :] = jnp.zeros((2, CARRY_ROWS, FF_BLOCK), F32)

    @pl.when(jnp.logical_not(seq_start))
    def _():
        hbuf[:, 0:CARRY_ROWS, :] = carry[j]

    _run_side_casts(refs[7:7 + n_jobs], refs[8 + n_jobs:-2])

    def block(width):
        x = x_ref[...]
        halves = []
        for s, (w_ref, cw_ref, cb_ref) in enumerate(((wg_ref, cwg_ref, cbg_ref), (wu_ref, cwu_ref, cbu_ref))):
            h = jnp.dot(x, w_ref[:, :width], preferred_element_type=F32)
            hbuf[s, CARRY_ROWS:, :width] = h
            carry[j, s, :, :width] = h[tm - CARRY_ROWS:, :]
            h1 = hbuf[s, CARRY_ROWS - 1:CARRY_ROWS - 1 + tm, :width]
            h2 = hbuf[s, CARRY_ROWS - 2:CARRY_ROWS - 2 + tm, :width]
            halves.append(cw_ref[2:3, :width] * h + cw_ref[1:2, :width] * h1 + cw_ref[0:1, :width] * h2
                          + cb_ref[:, :width])
        gate, up = halves
        o_ref[:, :width] = (gate / (1.0 + jnp.exp(-gate)) * up).astype(o_ref.dtype)

    if tail_width == FF_BLOCK:
        block(FF_BLOCK)
    else:
        pl.when(j < nj - 1)(lambda: block(FF_BLOCK))
        pl.when(j == nj - 1)(lambda: block(tail_width))


def _ffn_up(x, w_gate, w_up, cw_gate, cw_up, cb_gate, cb_up, seq, *, cast_jobs=(), tm=1024):
    m, k = x.shape
    d_ff = w_gate.shape[1]
    nj = pl.cdiv(d_ff, FF_BLOCK)
    col = lambda i, j: (0, j)
    c_in, c_out, c_shapes = _side_cast_plumbing(cast_jobs, (m // tm) * nj, lambda i, j: i * nj + j)
    res = pl.pallas_call(
        functools.partial(_ffn_up_kernel, tm=tm, nj=nj, tail_width=d_ff - (nj - 1) * FF_BLOCK,
                          tiles_per_seq=seq // tm, n_jobs=len(cast_jobs)),
        grid=(m // tm, nj),
        in_specs=[pl.BlockSpec((tm, k), lambda i, j: (i, 0), pipeline_mode=pl.Buffered(1)),
                  pl.BlockSpec((k, FF_BLOCK), col), pl.BlockSpec((k, FF_BLOCK), col),
                  pl.BlockSpec((CONV_WIDTH, FF_BLOCK), col), pl.BlockSpec((CONV_WIDTH, FF_BLOCK), col),
                  pl.BlockSpec((1, FF_BLOCK), col), pl.BlockSpec((1, FF_BLOCK), col)] + c_in,
        out_specs=[pl.BlockSpec((tm, FF_BLOCK), lambda i, j: (i, j))] + c_out,
        out_shape=[jax.ShapeDtypeStruct((m, d_ff), BF16)] + c_shapes,
        scratch_shapes=[pltpu.VMEM((2, tm + CARRY_ROWS, FF_BLOCK), F32),
                        pltpu.VMEM((nj, 2, CARRY_ROWS, FF_BLOCK), F32)],
        compiler_params=_params("arbitrary", "arbitrary"),
        name="ffn_up_conv_gate",
    )(x, w_gate, w_up, cw_gate, cw_up, cb_gate, cb_up, *[job[0] for job in cast_jobs])
    return res[0], res[1:]


def _proj_ln_kernel(*refs, nk, k_tail, n_chunks, n_jobs):
    a_ref, w_ref, x_ref, g_ref, b_ref = refs[:5]
    of_ref, ob_ref = refs[5 + n_jobs:7 + n_jobs]
    mu_ref, rstd_ref = refs[-2:]
    _run_side_casts(refs[5:5 + n_jobs], refs[7 + n_jobs:-2])
    k = pl.program_id(1)
    tm, n = of_ref.shape
    cw = n // n_chunks

    def add_residual(c):
        of_ref[:, c * cw:(c + 1) * cw] += DEEPNORM_ALPHA * x_ref[...]

    @pl.when(k == 0)
    def _():
        of_ref[...] = jnp.dot(a_ref[...], w_ref[...], preferred_element_type=F32)
        add_residual(0)

    for c in range(1, n_chunks):
        pl.when(k == c)(functools.partial(add_residual, c))

    @pl.when((k > 0) & (k < nk - 1))
    def _():
        of_ref[...] += jnp.dot(a_ref[...], w_ref[...], preferred_element_type=F32)

    @pl.when(k == nk - 1)
    def _():
        z = of_ref[...] + jnp.dot(a_ref[:, :k_tail], w_ref[:k_tail, :], preferred_element_type=F32)
        of_ref[...] = z
        d = z - z[:, 0:1]
        s1 = jnp.mean(d, axis=-1, keepdims=True)
        var = jnp.maximum(jnp.mean(d * d, axis=-1, keepdims=True) - s1 * s1, 0.0)
        mu_ref[...] = jnp.broadcast_to(z[:, 0:1] + s1, (tm, LANES))
        rstd_ref[...] = jnp.broadcast_to(lax.rsqrt(var + LN_EPS), (tm, LANES))

        def rows_at(r):
            return pl.ds(pl.multiple_of(r * LN_ROWS, LN_ROWS), LN_ROWS)

        def across(stat_ref, r):
            return jnp.concatenate([stat_ref[rows_at(r), :]] * (n // LANES), axis=1)

        def norm_pass(r, carry):
            d = of_ref[rows_at(r), :] - across(mu_ref, r)
            y = d * across(rstd_ref, r) * g_ref[...] + b_ref[...]
            of_ref[rows_at(r), :] = y
            ob_ref[rows_at(r), :] = y.astype(ob_ref.dtype)
            return carry

        lax.fori_loop(0, tm // LN_ROWS, norm_pass, 0, unroll=4)


def _proj_ln(a, w, x, g, b, *, name, cast_jobs=(), tk=1024):
    m, kdim = a.shape
    n = w.shape[1]
    nk = pl.cdiv(kdim, tk)
    k_tail = kdim - (nk - 1) * tk
    n_chunks = 1
    while n_chunks * 2 <= min(nk, 8):
        n_chunks *= 2
    cw = n // n_chunks
    tm = 1024 if n_chunks >= 4 else 512
    out_mode = pl.Buffered(1 if tm == 1024 else 2)
    c_in, c_out, c_shapes = _side_cast_plumbing(cast_jobs, (m // tm) * nk, lambda i, k: i * nk + k)
    res = pl.pallas_call(
        functools.partial(_proj_ln_kernel, nk=nk, k_tail=k_tail, n_chunks=n_chunks, n_jobs=len(cast_jobs)),
        grid=(m // tm, nk),
        in_specs=[pl.BlockSpec((tm, tk), lambda i, k: (i, k)),
                  pl.BlockSpec((tk, n), lambda i, k: (k, 0)),
                  pl.BlockSpec((tm, cw), lambda i, k: (i, jnp.minimum(k, n_chunks - 1))),
                  pl.BlockSpec((LN_ROWS, n), lambda i, k: (0, 0)),
                  pl.BlockSpec((LN_ROWS, n), lambda i, k: (0, 0))] + c_in,
        out_specs=[pl.BlockSpec((tm, n), lambda i, k: (i, 0), pipeline_mode=out_mode),
                   pl.BlockSpec((tm, n), lambda i, k: (i, 0), pipeline_mode=out_mode)] + c_out,
        out_shape=[jax.ShapeDtypeStruct((m, n), F32), jax.ShapeDtypeStruct((m, n), BF16)] + c_shapes,
        scratch_shapes=[pltpu.VMEM((tm, LANES), F32), pltpu.VMEM((tm, LANES), F32)],
        compiler_params=_params("arbitrary", "arbitrary"),
        name=name,
    )(a, w, x, jnp.broadcast_to(g.reshape(1, n), (LN_ROWS, n)), jnp.broadcast_to(b.reshape(1, n), (LN_ROWS, n)),
      *[job[0] for job in cast_jobs])
    return res[0], res[1], res[2:]


def _rotary_inv_freq():
    half = B_ROT_DIM // 2
    inv_freq = jnp.float32(ROPE_THETA) ** (-jnp.arange(half, dtype=F32) * 2.0 / B_ROT_DIM)
    zeros = jnp.zeros((B_HEAD_DIM - B_ROT_DIM,), F32)
    return jnp.concatenate([-inv_freq, inv_freq, zeros]).reshape(1, B_HEAD_DIM)


def kernel(x, positions, a_w_in, a_ln_g, a_ln_b, a_w_s, a_b_s, a_w_out, b_w_in, b_w_out, c_w_in, c_w_group, c_scale, c_w_out, f_w_up, f_conv_w, f_conv_b, f_w_down, ln_mix_g, ln_mix_b, ln_ffn_g, ln_ffn_b):
    batch, seq, d = x.shape
    m = batch * seq
    xf = x.reshape(m, d)
    xb = xf.astype(BF16)
    posb = jnp.broadcast_to(positions.reshape(m, 1).astype(F32), (m, B_HEAD_DIM))
    invf = _rotary_inv_freq()

    c_w_group_2d = c_w_group.reshape(c_w_group.shape[0], -1, c_w_group.shape[-1])
    a_in = [a_w_in[0].astype(BF16), None]
    a_out = [None, None]
    ffn_up_w = None
    for layer in range(DEPTH):
        kind, idx = layer % N_MIXERS, layer // N_MIXERS
        if kind == 0:
            jobs = [(f_w_up, layer, True)] + ([(a_w_out, 0, False)] if layer == 0 else [])
            h, casts = _mm(xb, a_in[idx], epilogue=_gelu, out_dtype=BF16, name="gmlp_in_gelu", cast_jobs=jobs)
            ffn_up_w = casts[:2]
            if layer == 0:
                a_out[0] = casts[2]
            bs_exp = jnp.repeat(a_b_s[idx].T, A_GROUP_DIM, axis=1)
            y = _spatial_gate(h, a_ln_g[idx].reshape(1, d), a_ln_b[idx].reshape(1, d), a_w_s[idx], bs_exp)
            w_out = a_out[idx]
        elif kind == 1:
            outs, lses, dilations = [], [], []
            for group, (window, dilation) in enumerate(B_PATTERNS):
                assert window // dilation == B_QBLOCK
                jobs = [(f_w_up, layer, True)] if group == 0 else []
                qkv, casts = _qkv_proj(xb, b_in, group, posb, invf, dilation, cast_jobs=jobs)
                if group == 0:
                    ffn_up_w = casts
                o, lse = _dilated_attention(qkv, seq, dilation)
                outs.append(o)
                lses.append(lse)
                dilations.append(dilation)
            y = _combine_groups(outs, lses, tuple(dilations), m)
            w_out = b_out
        else:
            h, ffn_up_w = _mm(xb, c_in, epilogue=lambda t: t, out_dtype=F32, name="pool_in",
                              cast_jobs=[(f_w_up, layer, True)])
            y = _pool_mix(h, c_group.reshape(c_w_group[idx].shape), c_scale[idx].reshape(1, d), seq)
            w_out = c_out
        xf, xb, _ = _proj_ln(y, w_out, xf, ln_mix_g[layer], ln_mix_b[layer], name="mixer_out_ln")

        jobs = [(f_w_down, layer, False)]
        nxt = layer + 1
        nxt_kind, nxt_idx = nxt % N_MIXERS, nxt // N_MIXERS
        if nxt < DEPTH and nxt_kind == 0:
            jobs += [(a_w_in, nxt_idx, False), (a_w_out, nxt_idx, False)]
        elif nxt < DEPTH and nxt_kind == 1:
            jobs += [(b_w_in, nxt_idx, False), (b_w_out, nxt_idx, False)]
        elif nxt < DEPTH:
            jobs += [(c_w_in, nxt_idx, False), (c_w_out, nxt_idx, False), (c_w_group_2d, nxt_idx, False)]
        cw, cb = f_conv_w[layer], f_conv_b[layer].reshape(1, -1)
        g, casts = _ffn_up(xb, ffn_up_w[0], ffn_up_w[1], cw[:, :D_FF], cw[:, D_FF:], cb[:, :D_FF], cb[:, D_FF:],
                           seq, cast_jobs=jobs)
        w_down = casts[0]
        if nxt < DEPTH and nxt_kind == 0:
            a_in[nxt_idx], a_out[nxt_idx] = casts[1:3]
        elif nxt < DEPTH and nxt_kind == 1:
            b_in, b_out = casts[1:3]
        elif nxt < DEPTH:
            c_in, c_out, c_group = casts[1:4]
        xf, xb, _ = _proj_ln(g, w_down, xf, ln_ffn_g[layer], ln_ffn_b[layer], name="ffn_down_ln")

    return xf.reshape(batch, seq, d)
```

```python
import functools
import math

import jax
import jax.numpy as jnp
from jax import lax
from jax.experimental import pallas as pl
from jax.experimental.pallas import tpu as pltpu

F32 = jnp.float32
BF16 = jnp.bfloat16

D_MODEL = 4096
DEPTH = 4
N_MIXERS = 3

A_CHUNK = 128
A_GROUPS = 16
A_GROUP_DIM = D_MODEL // A_GROUPS

B_PATTERNS = ((128, 1), (512, 4), (2048, 16))
B_HEADS = 16
B_HEAD_DIM = 128
B_ROT_DIM = B_HEAD_DIM // 4
B_QBLOCK = 128
B_WIDTH = B_HEADS * B_HEAD_DIM
ROPE_THETA = 500000.0

C_WINDOWS = (2, 4, 8, 16)
C_GROUP_DIM = D_MODEL // len(C_WINDOWS)
C_HALO = 16

D_FF = 11008
FF_BLOCK = 768
CONV_WIDTH = 3
CARRY_ROWS = 8

LN_EPS = 1e-5
LN_ROWS = 8
DEEPNORM_ALPHA = (2 * DEPTH) ** 0.25

LANES = 128
SIDE_CAST_WINDOW_BUDGET_BYTES = 12 * 1024 * 1024
V7X_VMEM_LIMIT_BYTES = 63 * 1024 * 1024


def _params(*semantics, flags=None):
    return pltpu.CompilerParams(dimension_semantics=semantics, vmem_limit_bytes=V7X_VMEM_LIMIT_BYTES,
                                flags=flags)


def _layer_norm(z, g, b):
    mu = jnp.mean(z, axis=-1, keepdims=True)
    d = z - mu
    var = jnp.mean(d * d, axis=-1, keepdims=True)
    return d * lax.rsqrt(var + LN_EPS) * g + b


def _cast_slab_rows(total_rows, n_steps, smallest=16):
    rows = smallest
    while total_rows % rows or total_rows // rows > n_steps:
        rows *= 2
        assert rows <= total_rows
    return rows


def _side_cast_plumbing(jobs, n_steps, step_of):
    in_specs, out_specs, out_shapes = [], [], []
    for src, layer, split in jobs:
        _, rows, width = src.shape
        slab = _cast_slab_rows(rows, n_steps)
        slab_of = lambda *g, last=rows // slab - 1: jnp.minimum(step_of(*g), last)
        in_specs.append(pl.BlockSpec((None, slab, width), lambda *g, f=slab_of, l=layer: (l, f(*g), 0)))
        parts = 2 if split else 1
        out_specs += [pl.BlockSpec((slab, width // parts), lambda *g, f=slab_of: (f(*g), 0))] * parts
        out_shapes += [jax.ShapeDtypeStruct((rows, width // parts), BF16)] * parts
    return in_specs, out_specs, out_shapes


def _host_column_block(jobs, row_tiles, n, tn):
    def window_bytes(tn):
        return sum(2 * _cast_slab_rows(src.shape[1], row_tiles * (n // tn)) * src.shape[2] * (4 + 2)
                   for src, _, _ in jobs)
    while window_bytes(tn) > SIDE_CAST_WINDOW_BUDGET_BYTES and tn > 2 * LANES:
        tn //= 2
    return tn


def _run_side_casts(src_refs, dst_refs):
    dst = list(dst_refs)
    for src in src_refs:
        width = dst[0].shape[1]
        for part in range(src.shape[1] // width):
            dst.pop(0)[...] = src[:, part * width:(part + 1) * width].astype(BF16)
    assert not dst


def _gelu(x):
    return 0.5 * x * (1.0 + lax.erf(x * (1.0 / math.sqrt(2.0))))


def _mm_kernel(*refs, epilogue, n_jobs):
    x_ref, w_ref = refs[:2]
    o_ref = refs[2 + n_jobs]
    _run_side_casts(refs[2:2 + n_jobs], refs[3 + n_jobs:])
    acc = jnp.dot(x_ref[...], w_ref[...], preferred_element_type=F32)
    o_ref[...] = epilogue(acc).astype(o_ref.dtype)


def _mm(x, w, *, epilogue, out_dtype, name, cast_jobs=(), tm=1024, tn=1024):
    m, k = x.shape
    n = w.shape[1]
    tn = _host_column_block(cast_jobs, m // tm, n, tn)
    nj = n // tn
    c_in, c_out, c_shapes = _side_cast_plumbing(cast_jobs, (m // tm) * nj, lambda i, j: i * nj + j)
    res = pl.pallas_call(
        functools.partial(_mm_kernel, epilogue=epilogue, n_jobs=len(cast_jobs)),
        grid=(m // tm, nj),
        in_specs=[pl.BlockSpec((tm, k), lambda i, j: (i, 0)),
                  pl.BlockSpec((k, tn), lambda i, j: (0, j))] + c_in,
        out_specs=[pl.BlockSpec((tm, tn), lambda i, j: (i, j))] + c_out,
        out_shape=[jax.ShapeDtypeStruct((m, n), out_dtype)] + c_shapes,
        compiler_params=_params("arbitrary", "arbitrary"),
        name=name,
    )(x, w, *[job[0] for job in cast_jobs])
    return res[0], res[1:]


def _qkv_kernel(*refs, tm, tn, dilation, n_jobs):
    x_ref, w_ref, pos_ref, invf_ref = refs[:4]
    o_ref = refs[4 + n_jobs]
    n_dst = len(refs) - (7 + n_jobs) - (dilation > 1)
    cos_ref, sin_ref, *perm_scratch = refs[5 + n_jobs + n_dst:]
    _run_side_casts(refs[4:4 + n_jobs], refs[5 + n_jobs:5 + n_jobs + n_dst])
    j = pl.program_id(1)

    @pl.when(j == 0)
    def _():
        ang = pos_ref[...] * invf_ref[...]
        cos_ref[...] = jnp.cos(ang)
        sin_ref[...] = jnp.sin(ang)

    def block(rotate):
        out = jnp.dot(x_ref[...], w_ref[...], preferred_element_type=F32)
        if rotate:
            reps = tn // B_HEAD_DIM
            cos = jnp.concatenate([cos_ref[...]] * reps, axis=1)
            sin = jnp.concatenate([sin_ref[...]] * reps, axis=1)
            lane = lax.broadcasted_iota(jnp.int32, out.shape, 1) & (B_HEAD_DIM - 1)
            half = B_ROT_DIM // 2
            partner = jnp.where(lane < half,
                                pltpu.roll(out, tn - half, 1),
                                pltpu.roll(out, half, 1))
            out = out * cos + partner * sin

        chunk = B_QBLOCK * dilation
        if dilation == 1:
            for c in range(tm // chunk):
                o_ref[c, 0] = out[c * chunk:(c + 1) * chunk].astype(o_ref.dtype)
            return
        (buf,) = perm_scratch
        for t in range(tn // LANES):
            cols = slice(t * LANES, (t + 1) * LANES)
            buf[t] = out[:, cols]
            if chunk <= tm:
                for c in range(tm // chunk):
                    for r in range(dilation):
                        rows = pl.ds(c * chunk + r, B_QBLOCK, stride=dilation)
                        o_ref[c, r, :, cols] = buf[t, rows, :].astype(o_ref.dtype)
            else:
                for r in range(dilation):
                    rows = pl.ds(r, tm // dilation, stride=dilation)
                    o_ref[r, :, cols] = buf[t, rows, :].astype(o_ref.dtype)

    is_v = (j * tn) // B_WIDTH == 2
    pl.when(jnp.logical_not(is_v))(lambda: block(True))
    pl.when(is_v)(lambda: block(False))


def _qkv_proj(x, w, group, posb, invf, dilation, *, cast_jobs=(), tm=1024, tn=1024):
    m, k = x.shape
    n = 3 * B_WIDTH
    tn = _host_column_block(cast_jobs, m // tm, n, tn)
    nj = n // tn
    first = group * nj
    c_in, c_out, c_shapes = _side_cast_plumbing(cast_jobs, (m // tm) * nj, lambda i, j: i * nj + j)
    chunk = B_QBLOCK * dilation
    if chunk <= tm:
        out_spec = pl.BlockSpec((tm // chunk, dilation, B_QBLOCK, tn), lambda i, j: (i, 0, 0, j))
    else:
        tiles_per_chunk = chunk // tm
        out_spec = pl.BlockSpec((None, dilation, tm // dilation, tn),
                                lambda i, j: (i // tiles_per_chunk, 0, i % tiles_per_chunk, j))
    scratch = [pltpu.VMEM((tm, B_HEAD_DIM), F32), pltpu.VMEM((tm, B_HEAD_DIM), F32)]
    if dilation > 1:
        scratch.append(pltpu.VMEM((tn // LANES, tm, LANES), F32))
    res = pl.pallas_call(
        functools.partial(_qkv_kernel, tm=tm, tn=tn, dilation=dilation, n_jobs=len(cast_jobs)),
        grid=(m // tm, nj),
        in_specs=[pl.BlockSpec((tm, k), lambda i, j: (i, 0)),
                  pl.BlockSpec((k, tn), lambda i, j: (0, first + j)),
                  pl.BlockSpec((tm, B_HEAD_DIM), lambda i, j: (i, 0)),
                  pl.BlockSpec((1, B_HEAD_DIM), lambda i, j: (0, 0))] + c_in,
        out_specs=[out_spec] + c_out,
        out_shape=[jax.ShapeDtypeStruct((m // chunk, dilation, B_QBLOCK, n), BF16)] + c_shapes,
        scratch_shapes=scratch,
        compiler_params=_params("arbitrary", "arbitrary"),
        name=f"qkv_proj_d{dilation}",
    )(x, w, posb, invf, *[job[0] for job in cast_jobs])
    return res[0], res[1:]


def _attn_kernel(q_ref, kp_ref, kc_ref, vp_ref, vc_ref, o_ref, l_ref, *, chunks_per_seq):
    c = pl.program_id(0)
    has_prev = (c % chunks_per_seq) > 0
    q_blk = B_QBLOCK
    qi = lax.broadcasted_iota(jnp.int32, (q_blk, 2 * q_blk), 0)
    kj = lax.broadcasted_iota(jnp.int32, (q_blk, 2 * q_blk), 1)
    mask = (kj >= qi) & (kj <= qi + q_blk) & ((kj >= q_blk) | has_prev)
    scale = B_HEAD_DIM ** -0.5
    head_lane = lax.broadcasted_iota(jnp.int32, (q_blk, LANES), 1)
    lse_tile = jnp.zeros((q_blk, LANES), F32)
    for h in range(B_HEADS):
        sl = slice(h * B_HEAD_DIM, (h + 1) * B_HEAD_DIM)
        q = q_ref[:, sl]
        k2 = jnp.concatenate([kp_ref[:, sl], kc_ref[:, sl]], axis=0)
        v2 = jnp.concatenate([vp_ref[:, sl], vc_ref[:, sl]], axis=0)
        s = lax.dot_general(q, k2, (((1,), (1,)), ((), ())), preferred_element_type=F32) * scale
        s = jnp.where(mask, s, -jnp.inf)
        mx = jnp.max(s, axis=-1, keepdims=True)
        p = jnp.exp(s - mx)
        den = jnp.sum(p, axis=-1, keepdims=True)
        o = jnp.dot(p.astype(BF16), v2, preferred_element_type=F32) / den
        o_ref[:, sl] = o.astype(o_ref.dtype)
        lse_tile = jnp.where(head_lane == h, mx + jnp.log(den), lse_tile)
    l_ref[...] = lse_tile


def _dilated_attention(qkv, seq, dilation):
    n_chunks = qkv.shape[0]
    chunks_per_seq = seq // (B_QBLOCK * dilation)

    def spec(which, prev):
        def index(c, r):
            return (jnp.maximum(c - 1, 0) if prev else c, r, 0, which)
        return pl.BlockSpec((None, None, B_QBLOCK, B_WIDTH), index)

    return pl.pallas_call(
        functools.partial(_attn_kernel, chunks_per_seq=chunks_per_seq),
        grid=(n_chunks, dilation),
        in_specs=[spec(0, False), spec(1, True), spec(1, False), spec(2, True), spec(2, False)],
        out_specs=[pl.BlockSpec((None, None, B_QBLOCK, B_WIDTH), lambda c, r: (c, r, 0, 0)),
                   pl.BlockSpec((None, None, B_QBLOCK, LANES), lambda c, r: (c, r, 0, 0))],
        out_shape=[jax.ShapeDtypeStruct((n_chunks, dilation, B_QBLOCK, B_WIDTH), BF16),
                   jax.ShapeDtypeStruct((n_chunks, dilation, B_QBLOCK, LANES), F32)],
        compiler_params=_params("parallel", "parallel"),
        name=f"dilated_attn_d{dilation}",
    )(qkv, qkv, qkv, qkv, qkv)


def _combine_kernel(*refs, tm, dilations):
    n = len(dilations)
    o_refs, l_refs, out_ref = refs[:n], refs[n:2 * n], refs[2 * n]
    o_bufs, l_bufs = refs[2 * n + 1:3 * n + 1], refs[3 * n + 1:4 * n + 1]

    def scatter(src_o, src_l, o_buf, l_buf, rows):
        l_buf[rows, :] = src_l
        for h in range(B_HEADS):
            o_buf[h, rows, :] = src_o[:, h * B_HEAD_DIM:(h + 1) * B_HEAD_DIM].astype(F32)

    for o_ref, l_ref, o_buf, l_buf, d in zip(o_refs, l_refs, o_bufs, l_bufs, dilations):
        chunk = B_QBLOCK * d
        if chunk <= tm:
            for c in range(tm // chunk):
                for r in range(d):
                    rows = pl.ds(c * chunk + r, B_QBLOCK, stride=d) if d > 1 else pl.ds(c * chunk, B_QBLOCK)
                    scatter(o_ref[c, r], l_ref[c, r], o_buf, l_buf, rows)
        else:
            for r in range(d):
                scatter(o_ref[r], l_ref[r], o_buf, l_buf, pl.ds(r, tm // d, stride=d))

    lses = [l_buf[...] for l_buf in l_bufs]
    mx = functools.reduce(jnp.maximum, lses)
    es = [jnp.exp(l - mx) for l in lses]
    tot = functools.reduce(lambda a, b: a + b, es)
    ws = [e / tot for e in es]
    for h in range(B_HEADS):
        acc = None
        for w, o_buf in zip(ws, o_bufs):
            term = jnp.broadcast_to(w[:, h:h + 1], (tm, B_HEAD_DIM)) * o_buf[h]
            acc = term if acc is None else acc + term
        out_ref[:, h * B_HEAD_DIM:(h + 1) * B_HEAD_DIM] = acc.astype(out_ref.dtype)


def _combine_groups(outs, lses, dilations, m, *, tm=512):
    def spec(d, width):
        chunk = B_QBLOCK * d
        if chunk <= tm:
            return pl.BlockSpec((tm // chunk, d, B_QBLOCK, width), lambda i: (i, 0, 0, 0))
        tiles_per_chunk = chunk // tm
        return pl.BlockSpec((None, d, tm // d, width),
                            lambda i: (i // tiles_per_chunk, 0, i % tiles_per_chunk, 0))

    n = len(dilations)
    return pl.pallas_call(
        functools.partial(_combine_kernel, tm=tm, dilations=dilations),
        grid=(m // tm,),
        in_specs=[spec(d, B_WIDTH) for d in dilations] + [spec(d, LANES) for d in dilations],
        out_specs=pl.BlockSpec((tm, B_WIDTH), lambda i: (i, 0)),
        out_shape=jax.ShapeDtypeStruct((m, B_WIDTH), BF16),
        scratch_shapes=[pltpu.VMEM((B_HEADS, tm, B_HEAD_DIM), F32)] * n + [pltpu.VMEM((tm, LANES), F32)] * n,
        compiler_params=_params("parallel"),
        name="attn_combine",
    )(*outs, *lses)


def _gate_kernel(u_ref, v_ref, g_ref, b_ref, ws_ref, bs_ref, o_ref, *, rows):
    vn = _layer_norm(v_ref[...].astype(F32), g_ref[...], b_ref[...]).astype(BF16)
    ti = lax.broadcasted_iota(jnp.int32, (A_CHUNK, A_CHUNK), 0)
    si = lax.broadcasted_iota(jnp.int32, (A_CHUNK, A_CHUNK), 1)
    causal = si <= ti
    for g in range(A_GROUPS):
        cols = slice(g * A_GROUP_DIM, (g + 1) * A_GROUP_DIM)
        w = jnp.where(causal, ws_ref[g], 0.0).astype(BF16)
        bias = bs_ref[:, cols]
        for c in range(rows // A_CHUNK):
            rws = slice(c * A_CHUNK, (c + 1) * A_CHUNK)
            mixed = jnp.dot(w, vn[rws, cols], preferred_element_type=F32) + bias
            o_ref[rws, cols] = (u_ref[rws, cols].astype(F32) * mixed).astype(o_ref.dtype)


def _spatial_gate(h, ln_g, ln_b, w_s, bs_exp, *, rows=256):
    m = h.shape[0]
    width = D_MODEL
    return pl.pallas_call(
        functools.partial(_gate_kernel, rows=rows),
        grid=(m // rows,),
        in_specs=[pl.BlockSpec((rows, width), lambda i: (i, 0)),
                  pl.BlockSpec((rows, width), lambda i: (i, 1)),
                  pl.BlockSpec((1, width), lambda i: (0, 0)),
                  pl.BlockSpec((1, width), lambda i: (0, 0)),
                  pl.BlockSpec((A_GROUPS, A_CHUNK, A_CHUNK), lambda i: (0, 0, 0)),
                  pl.BlockSpec((A_CHUNK, width), lambda i: (0, 0))],
        out_specs=pl.BlockSpec((rows, width), lambda i: (i, 0)),
        out_shape=jax.ShapeDtypeStruct((m, width), BF16),
        compiler_params=_params("parallel"),
        name="spatial_gate",
    )(h, h, ln_g, ln_b, w_s, bs_exp)


def _pool_kernel(h_ref, halo_ref, wg_ref, scale_ref, o_ref, *, rows, seq):
    i = pl.program_id(0)
    t0 = (i * rows) % seq
    pos = t0 + lax.broadcasted_iota(jnp.int32, (rows, 1), 0)
    keep_halo = t0 > 0
    for g, window in enumerate(C_WINDOWS):
        cols = slice(g * C_GROUP_DIM, (g + 1) * C_GROUP_DIM)
        hg = h_ref[:, cols]
        halo = jnp.where(keep_halo, halo_ref[:, cols], 0.0)
        acc = jnp.concatenate([halo, hg], axis=0)
        span = 1
        while span < window:
            acc = acc + pltpu.roll(acc, span, 0)
            span *= 2
        den = jnp.minimum(pos + 1, window).astype(F32)
        pooled = acc[C_HALO:, :] / den - hg
        y = jnp.dot(pooled.astype(BF16), wg_ref[g], preferred_element_type=F32) * scale_ref[:, cols]
        o_ref[:, cols] = y.astype(o_ref.dtype)


def _pool_mix(h, w_group, scale, seq, *, rows=256):
    m, width = h.shape
    halo_per_block = rows // C_HALO
    return pl.pallas_call(
        functools.partial(_pool_kernel, rows=rows, seq=seq),
        grid=(m // rows,),
        in_specs=[pl.BlockSpec((rows, width), lambda i: (i, 0)),
                  pl.BlockSpec((C_HALO, width), lambda i: (jnp.maximum(i * halo_per_block - 1, 0), 0)),
                  pl.BlockSpec((len(C_WINDOWS), C_GROUP_DIM, C_GROUP_DIM), lambda i: (0, 0, 0)),
                  pl.BlockSpec((1, width), lambda i: (0, 0))],
        out_specs=pl.BlockSpec((rows, width), lambda i: (i, 0)),
        out_shape=jax.ShapeDtypeStruct((m, width), BF16),
        compiler_params=_params("parallel"),
        name="pool_mix",
    )(h, h, w_group, scale)


def _ffn_up_kernel(*refs, tm, nj, tail_width, tiles_per_seq, n_jobs):
    x_ref, wg_ref, wu_ref, cwg_ref, cwu_ref, cbg_ref, cbu_ref = refs[:7]
    o_ref = refs[7 + n_jobs]
    hbuf, carry = refs[-2:]
    i = pl.program_id(0)
    j = pl.program_id(1)
    seq_start = (i % tiles_per_seq) == 0

    @pl.when((i == 0) & (j == 0))
    def _():
        carry[...] = jnp.zeros(carry.shape, F32)

    @pl.when(seq_start)
    def _():
        hbuf[:, 0:CARRY_ROWS, :] = jnp.zeros((2, CARRY_ROWS, FF_BLOCK), F32)

    @pl.when(jnp.logical_not(seq_start))
    def _():
        hbuf[:, 0:CARRY_ROWS, :] = carry[j]

    _run_side_casts(refs[7:7 + n_jobs], refs[8 + n_jobs:-2])

    def block(width):
        x = x_ref[...]
        halves = []
        for s, (w_ref, cw_ref, cb_ref) in enumerate(((wg_ref, cwg_ref, cbg_ref), (wu_ref, cwu_ref, cbu_ref))):
            h = jnp.dot(x, w_ref[:, :width], preferred_element_type=F32)
            hbuf[s, CARRY_ROWS:, :width] = h
            carry[j, s, :, :width] = h[tm - CARRY_ROWS:, :]
            h1 = hbuf[s, CARRY_ROWS - 1:CARRY_ROWS - 1 + tm, :width]
            h2 = hbuf[s, CARRY_ROWS - 2:CARRY_ROWS - 2 + tm, :width]
            halves.append(cw_ref[2:3, :width] * h + cw_ref[1:2, :width] * h1 + cw_ref[0:1, :width] * h2
                          + cb_ref[:, :width])
        gate, up = halves
        o_ref[:, :width] = (gate / (1.0 + jnp.exp(-gate)) * up).astype(o_ref.dtype)

    if tail_width == FF_BLOCK:
        block(FF_BLOCK)
    else:
        pl.when(j < nj - 1)(lambda: block(FF_BLOCK))
        pl.when(j == nj - 1)(lambda: block(tail_width))


def _ffn_up(x, w_gate, w_up, cw_gate, cw_up, cb_gate, cb_up, seq, *, cast_jobs=(), tm=1024):
    m, k = x.shape
    d_ff = w_gate.shape[1]
    nj = pl.cdiv(d_ff, FF_BLOCK)
    col = lambda i, j: (0, j)
    c_in, c_out, c_shapes = _side_cast_plumbing(cast_jobs, (m // tm) * nj, lambda i, j: i * nj + j)
    res = pl.pallas_call(
        functools.partial(_ffn_up_kernel, tm=tm, nj=nj, tail_width=d_ff - (nj - 1) * FF_BLOCK,
                          tiles_per_seq=seq // tm, n_jobs=len(cast_jobs)),
        grid=(m // tm, nj),
        in_specs=[pl.BlockSpec((tm, k), lambda i, j: (i, 0), pipeline_mode=pl.Buffered(1)),
                  pl.BlockSpec((k, FF_BLOCK), col), pl.BlockSpec((k, FF_BLOCK), col),
                  pl.BlockSpec((CONV_WIDTH, FF_BLOCK), col), pl.BlockSpec((CONV_WIDTH, FF_BLOCK), col),
                  pl.BlockSpec((1, FF_BLOCK), col), pl.BlockSpec((1, FF_BLOCK), col)] + c_in,
        out_specs=[pl.BlockSpec((tm, FF_BLOCK), lambda i, j: (i, j))] + c_out,
        out_shape=[jax.ShapeDtypeStruct((m, d_ff), BF16)] + c_shapes,
        scratch_shapes=[pltpu.VMEM((2, tm + CARRY_ROWS, FF_BLOCK), F32),
                        pltpu.VMEM((nj, 2, CARRY_ROWS, FF_BLOCK), F32)],
        compiler_params=_params("arbitrary", "arbitrary"),
        name="ffn_up_conv_gate",
    )(x, w_gate, w_up, cw_gate, cw_up, cb_gate, cb_up, *[job[0] for job in cast_jobs])
    return res[0], res[1:]


def _proj_ln_kernel(*refs, nk, k_tail, n_chunks, n_jobs):
    a_ref, w_ref, x_ref, g_ref, b_ref = refs[:5]
    of_ref, ob_ref = refs[5 + n_jobs:7 + n_jobs]
    mu_ref, rstd_ref = refs[-2:]
    _run_side_casts(refs[5:5 + n_jobs], refs[7 + n_jobs:-2])
    k = pl.program_id(1)
    tm, n = of_ref.shape
    cw = n // n_chunks

    def add_residual(c):
        of_ref[:, c * cw:(c + 1) * cw] += DEEPNORM_ALPHA * x_ref[...]

    @pl.when(k == 0)
    def _():
        of_ref[...] = jnp.dot(a_ref[...], w_ref[...], preferred_element_type=F32)
        add_residual(0)

    for c in range(1, n_chunks):
        pl.when(k == c)(functools.partial(add_residual, c))

    @pl.when((k > 0) & (k < nk - 1))
    def _():
        of_ref[...] += jnp.dot(a_ref[...], w_ref[...], preferred_element_type=F32)

    @pl.when(k == nk - 1)
    def _():
        z = of_ref[...] + jnp.dot(a_ref[:, :k_tail], w_ref[:k_tail, :], preferred_element_type=F32)
        of_ref[...] = z
        d = z - z[:, 0:1]
        s1 = jnp.mean(d, axis=-1, keepdims=True)
        var = jnp.maximum(jnp.mean(d * d, axis=-1, keepdims=True) - s1 * s1, 0.0)
        mu_ref[...] = jnp.broadcast_to(z[:, 0:1] + s1, (tm, LANES))
        rstd_ref[...] = jnp.broadcast_to(lax.rsqrt(var + LN_EPS), (tm, LANES))

        def rows_at(r):
            return pl.ds(pl.multiple_of(r * LN_ROWS, LN_ROWS), LN_ROWS)

        def across(stat_ref, r):
            return jnp.concatenate([stat_ref[rows_at(r), :]] * (n // LANES), axis=1)

        def norm_pass(r, carry):
            d = of_ref[rows_at(r), :] - across(mu_ref, r)
            y = d * across(rstd_ref, r) * g_ref[...] + b_ref[...]
            of_ref[rows_at(r), :] = y
            ob_ref[rows_at(r), :] = y.astype(ob_ref.dtype)
            return carry

        lax.fori_loop(0, tm // LN_ROWS, norm_pass, 0, unroll=4)


def _proj_ln(a, w, x, g, b, *, name, cast_jobs=(), tk=1024):
    m, kdim = a.shape
    n = w.shape[1]
    nk = pl.cdiv(kdim, tk)
    k_tail = kdim - (nk - 1) * tk
    n_chunks = 1
    while n_chunks * 2 <= min(nk, 8):
        n_chunks *= 2
    cw = n // n_chunks
    tm = 1024 if n_chunks >= 4 else 512
    out_mode = pl.Buffered(1 if tm == 1024 else 2)
    out_b_mode = pl.Buffered(1 if tm == 1024 and n_chunks < 8 else 2)
    c_in, c_out, c_shapes = _side_cast_plumbing(cast_jobs, (m // tm) * nk, lambda i, k: i * nk + k)
    res = pl.pallas_call(
        functools.partial(_proj_ln_kernel, nk=nk, k_tail=k_tail, n_chunks=n_chunks, n_jobs=len(cast_jobs)),
        grid=(m // tm, nk),
        in_specs=[pl.BlockSpec((tm, tk), lambda i, k: (i, k)),
                  pl.BlockSpec((tk, n), lambda i, k: (k, 0)),
                  pl.BlockSpec((tm, cw), lambda i, k: (i, jnp.minimum(k, n_chunks - 1))),
                  pl.BlockSpec((LN_ROWS, n), lambda i, k: (0, 0)),
                  pl.BlockSpec((LN_ROWS, n), lambda i, k: (0, 0))] + c_in,
        out_specs=[pl.BlockSpec((tm, n), lambda i, k: (i, 0), pipeline_mode=out_mode),
                   pl.BlockSpec((tm, n), lambda i, k: (i, 0), pipeline_mode=out_b_mode)] + c_out,
        out_shape=[jax.ShapeDtypeStruct((m, n), F32), jax.ShapeDtypeStruct((m, n), BF16)] + c_shapes,
        scratch_shapes=[pltpu.VMEM((tm, LANES), F32), pltpu.VMEM((tm, LANES), F32)],
        compiler_params=_params("arbitrary", "arbitrary"),
        name=name,
    )(a, w, x, jnp.broadcast_to(g.reshape(1, n), (LN_ROWS, n)), jnp.broadcast_to(b.reshape(1, n), (LN_ROWS, n)),
      *[job[0] for job in cast_jobs])
    return res[0], res[1], res[2:]


def _rotary_inv_freq():
    half = B_ROT_DIM // 2
    inv_freq = jnp.float32(ROPE_THETA) ** (-jnp.arange(half, dtype=F32) * 2.0 / B_ROT_DIM)
    zeros = jnp.zeros((B_HEAD_DIM - B_ROT_DIM,), F32)
    return jnp.concatenate([-inv_freq, inv_freq, zeros]).reshape(1, B_HEAD_DIM)


def kernel(x, positions, a_w_in, a_ln_g, a_ln_b, a_w_s, a_b_s, a_w_out, b_w_in, b_w_out, c_w_in, c_w_group, c_scale, c_w_out, f_w_up, f_conv_w, f_conv_b, f_w_down, ln_mix_g, ln_mix_b, ln_ffn_g, ln_ffn_b):
    batch, seq, d = x.shape
    m = batch * seq
    xf = x.reshape(m, d)
    xb = xf.astype(BF16)
    posb = jnp.broadcast_to(positions.reshape(m, 1).astype(F32), (m, B_HEAD_DIM))
    invf = _rotary_inv_freq()

    c_w_group_2d = c_w_group.reshape(c_w_group.shape[0], -1, c_w_group.shape[-1])
    a_in = [a_w_in[0].astype(BF16), None]
    a_out = [None, None]
    ffn_up_w = None
    for layer in range(DEPTH):
        kind, idx = layer % N_MIXERS, layer // N_MIXERS
        if kind == 0:
            jobs = [(f_w_up, layer, True)] + ([(a_w_out, 0, False)] if layer == 0 else [])
            h, casts = _mm(xb, a_in[idx], epilogue=_gelu, out_dtype=BF16, name="gmlp_in_gelu", cast_jobs=jobs)
            ffn_up_w = casts[:2]
            if layer == 0:
                a_out[0] = casts[2]
            bs_exp = jnp.repeat(a_b_s[idx].T, A_GROUP_DIM, axis=1)
            y = _spatial_gate(h, a_ln_g[idx].reshape(1, d), a_ln_b[idx].reshape(1, d), a_w_s[idx], bs_exp)
            w_out = a_out[idx]
        elif kind == 1:
            outs, lses, dilations = [], [], []
            for group, (window, dilation) in enumerate(B_PATTERNS):
                assert window // dilation == B_QBLOCK
                jobs = [(f_w_up, layer, True)] if group == 0 else []
                qkv, casts = _qkv_proj(xb, b_in, group, posb, invf, dilation, cast_jobs=jobs)
                if group == 0:
                    ffn_up_w = casts
                o, lse = _dilated_attention(qkv, seq, dilation)
                outs.append(o)
                lses.append(lse)
                dilations.append(dilation)
            y = _combine_groups(outs, lses, tuple(dilations), m)
            w_out = b_out
        else:
            h, ffn_up_w = _mm(xb, c_in, epilogue=lambda t: t, out_dtype=F32, name="pool_in",
                              cast_jobs=[(f_w_up, layer, True)])
            y = _pool_mix(h, c_group.reshape(c_w_group[idx].shape), c_scale[idx].reshape(1, d), seq)
            w_out = c_out
        xf, xb, _ = _proj_ln(y, w_out, xf, ln_mix_g[layer], ln_mix_b[layer], name="mixer_out_ln")

        jobs = [(f_w_down, layer, False)]
        nxt = layer + 1
        nxt_kind, nxt_idx = nxt % N_MIXERS, nxt // N_MIXERS
        if nxt < DEPTH and nxt_kind == 0:
            jobs += [(a_w_in, nxt_idx, False), (a_w_out, nxt_idx, False)]
        elif nxt < DEPTH and nxt_kind == 1:
            jobs += [(b_w_in, nxt_idx, False), (b_w_out, nxt_idx, False)]
        elif nxt < DEPTH:
            jobs += [(c_w_in, nxt_idx, False), (c_w_out, nxt_idx, False), (c_w_group_2d, nxt_idx, False)]
        cw, cb = f_conv_w[layer], f_conv_b[layer].reshape(1, -1)
        g, casts = _ffn_up(xb, ffn_up_w[0], ffn_up_w[1], cw[:, :D_FF], cw[:, D_FF:], cb[:, :D_FF], cb[:, D_FF:],
                           seq, cast_jobs=jobs)
        w_down = casts[0]
        if nxt < DEPTH and nxt_kind == 0:
            a_in[nxt_idx], a_out[nxt_idx] = casts[1:3]
        elif nxt < DEPTH and nxt_kind == 1:
            b_in, b_out = casts[1:3]
        elif nxt < DEPTH:
            c_in, c_out, c_group = casts[1:4]
        xf, xb, _ = _proj_ln(g, w_down, xf, ln_ffn_g[layer], ln_ffn_b[layer], name="ffn_down_ln")

    return xf.reshape(batch, seq, d)
```

```python
import functools
import math

import jax
import jax.numpy as jnp
from jax import lax
from jax.experimental import pallas as pl
from jax.experimental.pallas import tpu as pltpu

F32 = jnp.float32
BF16 = jnp.bfloat16

D_MODEL = 4096
DEPTH = 4
N_MIXERS = 3

A_CHUNK = 128
A_GROUPS = 16
A_GROUP_DIM = D_MODEL // A_GROUPS

B_PATTERNS = ((128, 1), (512, 4), (2048, 16))
B_HEADS = 16
B_HEAD_DIM = 128
B_ROT_DIM = B_HEAD_DIM // 4
B_QBLOCK = 128
B_WIDTH = B_HEADS * B_HEAD_DIM
ROPE_THETA = 500000.0

C_WINDOWS = (2, 4, 8, 16)
C_GROUP_DIM = D_MODEL // len(C_WINDOWS)
C_HALO = 16

D_FF = 11008
FF_BLOCK = 768
CONV_WIDTH = 3
CARRY_ROWS = 8

LN_EPS = 1e-5
LN_ROWS = 8
DEEPNORM_ALPHA = (2 * DEPTH) ** 0.25

LANES = 128
SIDE_CAST_WINDOW_BUDGET_BYTES = 12 * 1024 * 1024
V7X_VMEM_LIMIT_BYTES = 63 * 1024 * 1024


def _params(*semantics, flags=None):
    return pltpu.CompilerParams(dimension_semantics=semantics, vmem_limit_bytes=V7X_VMEM_LIMIT_BYTES,
                                flags=flags)


def _layer_norm(z, g, b):
    mu = jnp.mean(z, axis=-1, keepdims=True)
    d = z - mu
    var = jnp.mean(d * d, axis=-1, keepdims=True)
    return d * lax.rsqrt(var + LN_EPS) * g + b


def _cast_slab_rows(total_rows, n_steps, smallest=16):
    rows = smallest
    while total_rows % rows or total_rows // rows > n_steps:
        rows *= 2
        assert rows <= total_rows
    return rows


def _side_cast_plumbing(jobs, n_steps, step_of):
    in_specs, out_specs, out_shapes = [], [], []
    for src, layer, split in jobs:
        _, rows, width = src.shape
        slab = _cast_slab_rows(rows, n_steps)
        slab_of = lambda *g, last=rows // slab - 1: jnp.minimum(step_of(*g), last)
        in_specs.append(pl.BlockSpec((None, slab, width), lambda *g, f=slab_of, l=layer: (l, f(*g), 0)))
        parts = 2 if split else 1
        out_specs += [pl.BlockSpec((slab, width // parts), lambda *g, f=slab_of: (f(*g), 0))] * parts
        out_shapes += [jax.ShapeDtypeStruct((rows, width // parts), BF16)] * parts
    return in_specs, out_specs, out_shapes


def _host_column_block(jobs, row_tiles, n, tn):
    def window_bytes(tn):
        return sum(2 * _cast_slab_rows(src.shape[1], row_tiles * (n // tn)) * src.shape[2] * (4 + 2)
                   for src, _, _ in jobs)
    while window_bytes(tn) > SIDE_CAST_WINDOW_BUDGET_BYTES and tn > 2 * LANES:
        tn //= 2
    return tn


def _run_side_casts(src_refs, dst_refs):
    dst = list(dst_refs)
    for src in src_refs:
        width = dst[0].shape[1]
        for part in range(src.shape[1] // width):
            dst.pop(0)[...] = src[:, part * width:(part + 1) * width].astype(BF16)
    assert not dst


def _gelu(x):
    return 0.5 * x * (1.0 + lax.erf(x * (1.0 / math.sqrt(2.0))))


def _mm_kernel(*refs, epilogue, n_jobs):
    x_ref, w_ref = refs[:2]
    o_ref = refs[2 + n_jobs]
    _run_side_casts(refs[2:2 + n_jobs], refs[3 + n_jobs:])
    acc = jnp.dot(x_ref[...], w_ref[...], preferred_element_type=F32)
    o_ref[...] = epilogue(acc).astype(o_ref.dtype)


def _mm(x, w, *, epilogue, out_dtype, name, cast_jobs=(), tm=1024, tn=1024):
    m, k = x.shape
    n = w.shape[1]
    tn = _host_column_block(cast_jobs, m // tm, n, tn)
    nj = n // tn
    c_in, c_out, c_shapes = _side_cast_plumbing(cast_jobs, (m // tm) * nj, lambda i, j: i * nj + j)
    res = pl.pallas_call(
        functools.partial(_mm_kernel, epilogue=epilogue, n_jobs=len(cast_jobs)),
        grid=(m // tm, nj),
        in_specs=[pl.BlockSpec((tm, k), lambda i, j: (i, 0)),
                  pl.BlockSpec((k, tn), lambda i, j: (0, j))] + c_in,
        out_specs=[pl.BlockSpec((tm, tn), lambda i, j: (i, j))] + c_out,
        out_shape=[jax.ShapeDtypeStruct((m, n), out_dtype)] + c_shapes,
        compiler_params=_params("arbitrary", "arbitrary"),
        name=name,
    )(x, w, *[job[0] for job in cast_jobs])
    return res[0], res[1:]


def _qkv_kernel(*refs, tm, tn, dilation, n_jobs):
    x_ref, w_ref, pos_ref, invf_ref = refs[:4]
    o_ref = refs[4 + n_jobs]
    n_dst = len(refs) - (7 + n_jobs) - (dilation > 1)
    cos_ref, sin_ref, *perm_scratch = refs[5 + n_jobs + n_dst:]
    _run_side_casts(refs[4:4 + n_jobs], refs[5 + n_jobs:5 + n_jobs + n_dst])
    j = pl.program_id(1)

    @pl.when(j == 0)
    def _():
        ang = pos_ref[...] * invf_ref[...]
        cos_ref[...] = jnp.cos(ang)
        sin_ref[...] = jnp.sin(ang)

    def block(rotate):
        out = jnp.dot(x_ref[...], w_ref[...], preferred_element_type=F32)
        if rotate:
            reps = tn // B_HEAD_DIM
            cos = jnp.concatenate([cos_ref[...]] * reps, axis=1)
            sin = jnp.concatenate([sin_ref[...]] * reps, axis=1)
            lane = lax.broadcasted_iota(jnp.int32, out.shape, 1) & (B_HEAD_DIM - 1)
            half = B_ROT_DIM // 2
            partner = jnp.where(lane < half,
                                pltpu.roll(out, tn - half, 1),
                                pltpu.roll(out, half, 1))
            out = out * cos + partner * sin

        chunk = B_QBLOCK * dilation
        if dilation == 1:
            for c in range(tm // chunk):
                o_ref[c, 0] = out[c * chunk:(c + 1) * chunk].astype(o_ref.dtype)
            return
        (buf,) = perm_scratch
        for t in range(tn // LANES):
            cols = slice(t * LANES, (t + 1) * LANES)
            buf[t] = out[:, cols]
            if chunk <= tm:
                for c in range(tm // chunk):
                    for r in range(dilation):
                        rows = pl.ds(c * chunk + r, B_QBLOCK, stride=dilation)
                        o_ref[c, r, :, cols] = buf[t, rows, :].astype(o_ref.dtype)
            else:
                for r in range(dilation):
                    rows = pl.ds(r, tm // dilation, stride=dilation)
                    o_ref[r, :, cols] = buf[t, rows, :].astype(o_ref.dtype)

    is_v = (j * tn) // B_WIDTH == 2
    pl.when(jnp.logical_not(is_v))(lambda: block(True))
    pl.when(is_v)(lambda: block(False))


def _qkv_proj(x, w, group, posb, invf, dilation, *, cast_jobs=(), tm=1024, tn=1024):
    m, k = x.shape
    n = 3 * B_WIDTH
    tn = _host_column_block(cast_jobs, m // tm, n, tn)
    nj = n // tn
    first = group * nj
    c_in, c_out, c_shapes = _side_cast_plumbing(cast_jobs, (m // tm) * nj, lambda i, j: i * nj + j)
    chunk = B_QBLOCK * dilation
    if chunk <= tm:
        out_spec = pl.BlockSpec((tm // chunk, dilation, B_QBLOCK, tn), lambda i, j: (i, 0, 0, j))
    else:
        tiles_per_chunk = chunk // tm
        out_spec = pl.BlockSpec((None, dilation, tm // dilation, tn),
                                lambda i, j: (i // tiles_per_chunk, 0, i % tiles_per_chunk, j))
    scratch = [pltpu.VMEM((tm, B_HEAD_DIM), F32), pltpu.VMEM((tm, B_HEAD_DIM), F32)]
    if dilation > 1:
        scratch.append(pltpu.VMEM((tn // LANES, tm, LANES), F32))
    res = pl.pallas_call(
        functools.partial(_qkv_kernel, tm=tm, tn=tn, dilation=dilation, n_jobs=len(cast_jobs)),
        grid=(m // tm, nj),
        in_specs=[pl.BlockSpec((tm, k), lambda i, j: (i, 0)),
                  pl.BlockSpec((k, tn), lambda i, j: (0, first + j)),
                  pl.BlockSpec((tm, B_HEAD_DIM), lambda i, j: (i, 0)),
                  pl.BlockSpec((1, B_HEAD_DIM), lambda i, j: (0, 0))] + c_in,
        out_specs=[out_spec] + c_out,
        out_shape=[jax.ShapeDtypeStruct((m // chunk, dilation, B_QBLOCK, n), BF16)] + c_shapes,
        scratch_shapes=scratch,
        compiler_params=_params("arbitrary", "arbitrary"),
        name=f"qkv_proj_d{dilation}",
    )(x, w, posb, invf, *[job[0] for job in cast_jobs])
    return res[0], res[1:]


def _attn_kernel(*refs, chunks_per_seq, n_jobs):
    q_ref, kp_ref, kc_ref, vp_ref, vc_ref = refs[:5]
    o_ref, l_ref = refs[5 + n_jobs:7 + n_jobs]
    _run_side_casts(refs[5:5 + n_jobs], refs[7 + n_jobs:])
    c = pl.program_id(0)
    has_prev = (c % chunks_per_seq) > 0
    q_blk = B_QBLOCK
    qi = lax.broadcasted_iota(jnp.int32, (q_blk, 2 * q_blk), 0)
    kj = lax.broadcasted_iota(jnp.int32, (q_blk, 2 * q_blk), 1)
    mask = (kj >= qi) & (kj <= qi + q_blk) & ((kj >= q_blk) | has_prev)
    scale = B_HEAD_DIM ** -0.5
    head_lane = lax.broadcasted_iota(jnp.int32, (q_blk, LANES), 1)
    lse_tile = jnp.zeros((q_blk, LANES), F32)
    for h in range(B_HEADS):
        sl = slice(h * B_HEAD_DIM, (h + 1) * B_HEAD_DIM)
        q = q_ref[:, sl]
        k2 = jnp.concatenate([kp_ref[:, sl], kc_ref[:, sl]], axis=0)
        v2 = jnp.concatenate([vp_ref[:, sl], vc_ref[:, sl]], axis=0)
        s = lax.dot_general(q, k2, (((1,), (1,)), ((), ())), preferred_element_type=F32) * scale
        s = jnp.where(mask, s, -jnp.inf)
        mx = jnp.max(s, axis=-1, keepdims=True)
        p = jnp.exp(s - mx)
        den = jnp.sum(p, axis=-1, keepdims=True)
        o = jnp.dot(p.astype(BF16), v2, preferred_element_type=F32) / den
        o_ref[:, sl] = o.astype(o_ref.dtype)
        lse_tile = jnp.where(head_lane == h, mx + jnp.log(den), lse_tile)
    l_ref[...] = lse_tile


def _dilated_attention(qkv, seq, dilation, *, cast_jobs=()):
    n_chunks = qkv.shape[0]
    chunks_per_seq = seq // (B_QBLOCK * dilation)

    def spec(which, prev):
        def index(c, r):
            return (jnp.maximum(c - 1, 0) if prev else c, r, 0, which)
        return pl.BlockSpec((None, None, B_QBLOCK, B_WIDTH), index)

    c_in, c_out, c_shapes = _side_cast_plumbing(cast_jobs, n_chunks * dilation, lambda c, r: c * dilation + r)
    res = pl.pallas_call(
        functools.partial(_attn_kernel, chunks_per_seq=chunks_per_seq, n_jobs=len(cast_jobs)),
        grid=(n_chunks, dilation),
        in_specs=[spec(0, False), spec(1, True), spec(1, False), spec(2, True), spec(2, False)] + c_in,
        out_specs=[pl.BlockSpec((None, None, B_QBLOCK, B_WIDTH), lambda c, r: (c, r, 0, 0)),
                   pl.BlockSpec((None, None, B_QBLOCK, LANES), lambda c, r: (c, r, 0, 0))] + c_out,
        out_shape=[jax.ShapeDtypeStruct((n_chunks, dilation, B_QBLOCK, B_WIDTH), BF16),
                   jax.ShapeDtypeStruct((n_chunks, dilation, B_QBLOCK, LANES), F32)] + c_shapes,
        compiler_params=_params("arbitrary", "arbitrary"),
        name=f"dilated_attn_d{dilation}",
    )(qkv, qkv, qkv, qkv, qkv, *[job[0] for job in cast_jobs])
    return res[0], res[1], res[2:]


def _combine_kernel(*refs, tm, dilations):
    n = len(dilations)
    o_refs, l_refs, out_ref = refs[:n], refs[n:2 * n], refs[2 * n]
    o_bufs, l_bufs = refs[2 * n + 1:3 * n + 1], refs[3 * n + 1:4 * n + 1]

    def scatter(src_o, src_l, o_buf, l_buf, rows):
        l_buf[rows, :] = src_l
        for h in range(B_HEADS):
            o_buf[h, rows, :] = src_o[:, h * B_HEAD_DIM:(h + 1) * B_HEAD_DIM].astype(F32)

    for o_ref, l_ref, o_buf, l_buf, d in zip(o_refs, l_refs, o_bufs, l_bufs, dilations):
        chunk = B_QBLOCK * d
        if chunk <= tm:
            for c in range(tm // chunk):
                for r in range(d):
                    rows = pl.ds(c * chunk + r, B_QBLOCK, stride=d) if d > 1 else pl.ds(c * chunk, B_QBLOCK)
                    scatter(o_ref[c, r], l_ref[c, r], o_buf, l_buf, rows)
        else:
            for r in range(d):
                scatter(o_ref[r], l_ref[r], o_buf, l_buf, pl.ds(r, tm // d, stride=d))

    lses = [l_buf[...] for l_buf in l_bufs]
    mx = functools.reduce(jnp.maximum, lses)
    es = [jnp.exp(l - mx) for l in lses]
    tot = functools.reduce(lambda a, b: a + b, es)
    ws = [e / tot for e in es]
    for h in range(B_HEADS):
        acc = None
        for w, o_buf in zip(ws, o_bufs):
            term = jnp.broadcast_to(w[:, h:h + 1], (tm, B_HEAD_DIM)) * o_buf[h]
            acc = term if acc is None else acc + term
        out_ref[:, h * B_HEAD_DIM:(h + 1) * B_HEAD_DIM] = acc.astype(out_ref.dtype)


def _combine_groups(outs, lses, dilations, m, *, tm=512):
    def spec(d, width):
        chunk = B_QBLOCK * d
        if chunk <= tm:
            return pl.BlockSpec((tm // chunk, d, B_QBLOCK, width), lambda i: (i, 0, 0, 0))
        tiles_per_chunk = chunk // tm
        return pl.BlockSpec((None, d, tm // d, width),
                            lambda i: (i // tiles_per_chunk, 0, i % tiles_per_chunk, 0))

    n = len(dilations)
    return pl.pallas_call(
        functools.partial(_combine_kernel, tm=tm, dilations=dilations),
        grid=(m // tm,),
        in_specs=[spec(d, B_WIDTH) for d in dilations] + [spec(d, LANES) for d in dilations],
        out_specs=pl.BlockSpec((tm, B_WIDTH), lambda i: (i, 0)),
        out_shape=jax.ShapeDtypeStruct((m, B_WIDTH), BF16),
        scratch_shapes=[pltpu.VMEM((B_HEADS, tm, B_HEAD_DIM), F32)] * n + [pltpu.VMEM((tm, LANES), F32)] * n,
        compiler_params=_params("parallel"),
        name="attn_combine",
    )(*outs, *lses)


def _gate_kernel(u_ref, v_ref, g_ref, b_ref, ws_ref, bs_ref, o_ref, *, rows):
    vn = _layer_norm(v_ref[...].astype(F32), g_ref[...], b_ref[...]).astype(BF16)
    ti = lax.broadcasted_iota(jnp.int32, (A_CHUNK, A_CHUNK), 0)
    si = lax.broadcasted_iota(jnp.int32, (A_CHUNK, A_CHUNK), 1)
    causal = si <= ti
    for g in range(A_GROUPS):
        cols = slice(g * A_GROUP_DIM, (g + 1) * A_GROUP_DIM)
        w = jnp.where(causal, ws_ref[g], 0.0).astype(BF16)
        bias = bs_ref[:, cols]
        for c in range(rows // A_CHUNK):
            rws = slice(c * A_CHUNK, (c + 1) * A_CHUNK)
            mixed = jnp.dot(w, vn[rws, cols], preferred_element_type=F32) + bias
            o_ref[rws, cols] = (u_ref[rws, cols].astype(F32) * mixed).astype(o_ref.dtype)


def _spatial_gate(h, ln_g, ln_b, w_s, bs_exp, *, rows=256):
    m = h.shape[0]
    width = D_MODEL
    return pl.pallas_call(
        functools.partial(_gate_kernel, rows=rows),
        grid=(m // rows,),
        in_specs=[pl.BlockSpec((rows, width), lambda i: (i, 0)),
                  pl.BlockSpec((rows, width), lambda i: (i, 1)),
                  pl.BlockSpec((1, width), lambda i: (0, 0)),
                  pl.BlockSpec((1, width), lambda i: (0, 0)),
                  pl.BlockSpec((A_GROUPS, A_CHUNK, A_CHUNK), lambda i: (0, 0, 0)),
                  pl.BlockSpec((A_CHUNK, width), lambda i: (0, 0))],
        out_specs=pl.BlockSpec((rows, width), lambda i: (i, 0)),
        out_shape=jax.ShapeDtypeStruct((m, width), BF16),
        compiler_params=_params("parallel"),
        name="spatial_gate",
    )(h, h, ln_g, ln_b, w_s, bs_exp)


def _pool_kernel(h_ref, halo_ref, wg_ref, scale_ref, o_ref, *, rows, seq):
    i = pl.program_id(0)
    t0 = (i * rows) % seq
    pos = t0 + lax.broadcasted_iota(jnp.int32, (rows, 1), 0)
    keep_halo = t0 > 0
    for g, window in enumerate(C_WINDOWS):
        cols = slice(g * C_GROUP_DIM, (g + 1) * C_GROUP_DIM)
        hg = h_ref[:, cols]
        halo = jnp.where(keep_halo, halo_ref[:, cols], 0.0)
        acc = jnp.concatenate([halo, hg], axis=0)
        span = 1
        while span < window:
            acc = acc + pltpu.roll(acc, span, 0)
            span *= 2
        den = jnp.minimum(pos + 1, window).astype(F32)
        pooled = acc[C_HALO:, :] / den - hg
        y = jnp.dot(pooled.astype(BF16), wg_ref[g], preferred_element_type=F32) * scale_ref[:, cols]
        o_ref[:, cols] = y.astype(o_ref.dtype)


def _pool_mix(h, w_group, scale, seq, *, rows=256):
    m, width = h.shape
    halo_per_block = rows // C_HALO
    return pl.pallas_call(
        functools.partial(_pool_kernel, rows=rows, seq=seq),
        grid=(m // rows,),
        in_specs=[pl.BlockSpec((rows, width), lambda i: (i, 0)),
                  pl.BlockSpec((C_HALO, width), lambda i: (jnp.maximum(i * halo_per_block - 1, 0), 0)),
                  pl.BlockSpec((len(C_WINDOWS), C_GROUP_DIM, C_GROUP_DIM), lambda i: (0, 0, 0)),
                  pl.BlockSpec((1, width), lambda i: (0, 0))],
        out_specs=pl.BlockSpec((rows, width), lambda i: (i, 0)),
        out_shape=jax.ShapeDtypeStruct((m, width), BF16),
        compiler_params=_params("parallel"),
        name="pool_mix",
    )(h, h, w_group, scale)


def _ffn_up_kernel(*refs, tm, nj, tail_width, tiles_per_seq, n_jobs):
    x_ref, wg_ref, wu_ref, cwg_ref, cwu_ref, cbg_ref, cbu_ref = refs[:7]
    o_ref = refs[7 + n_jobs]
    hbuf, carry = refs[-2:]
    i = pl.program_id(0)
    j = pl.program_id(1)
    seq_start = (i % tiles_per_seq) == 0

    @pl.when((i == 0) & (j == 0))
    def _():
        carry[...] = jnp.zeros(carry.shape, F32)

    @pl.when(seq_start)
    def _():
        hbuf[:, 0:CARRY_ROWS, :] = jnp.zeros((2, CARRY_ROWS, FF_BLOCK), F32)

    @pl.when(jnp.logical_not(seq_start))
    def _():
        hbuf[:, 0:CARRY_ROWS, :] = carry[j]

    _run_side_casts(refs[7:7 + n_jobs], refs[8 + n_jobs:-2])

    def block(width):
        x = x_ref[...]
        halves = []
        for s, (w_ref, cw_ref, cb_ref) in enumerate(((wg_ref, cwg_ref, cbg_ref), (wu_ref, cwu_ref, cbu_ref))):
            h = jnp.dot(x, w_ref[:, :width], preferred_element_type=F32)
            hbuf[s, CARRY_ROWS:, :width] = h
            carry[j, s, :, :width] = h[tm - CARRY_ROWS:, :]
            h1 = hbuf[s, CARRY_ROWS - 1:CARRY_ROWS - 1 + tm, :width]
            h2 = hbuf[s, CARRY_ROWS - 2:CARRY_ROWS - 2 + tm, :width]
            halves.append(cw_ref[2:3, :width] * h + cw_ref[1:2, :width] * h1 + cw_ref[0:1, :width] * h2
                          + cb_ref[:, :width])
        gate, up = halves
        o_ref[:, :width] = (gate / (1.0 + jnp.exp(-gate)) * up).astype(o_ref.dtype)

    if tail_width == FF_BLOCK:
        block(FF_BLOCK)
    else:
        pl.when(j < nj - 1)(lambda: block(FF_BLOCK))
        pl.when(j == nj - 1)(lambda: block(tail_width))


def _ffn_up(x, w_gate, w_up, cw_gate, cw_up, cb_gate, cb_up, seq, *, cast_jobs=(), tm=1024):
    m, k = x.shape
    d_ff = w_gate.shape[1]
    nj = pl.cdiv(d_ff, FF_BLOCK)
    col = lambda i, j: (0, j)
    c_in, c_out, c_shapes = _side_cast_plumbing(cast_jobs, (m // tm) * nj, lambda i, j: i * nj + j)
    res = pl.pallas_call(
        functools.partial(_ffn_up_kernel, tm=tm, nj=nj, tail_width=d_ff - (nj - 1) * FF_BLOCK,
                          tiles_per_seq=seq // tm, n_jobs=len(cast_jobs)),
        grid=(m // tm, nj),
        in_specs=[pl.BlockSpec((tm, k), lambda i, j: (i, 0), pipeline_mode=pl.Buffered(1)),
                  pl.BlockSpec((k, FF_BLOCK), col), pl.BlockSpec((k, FF_BLOCK), col),
                  pl.BlockSpec((CONV_WIDTH, FF_BLOCK), col), pl.BlockSpec((CONV_WIDTH, FF_BLOCK), col),
                  pl.BlockSpec((1, FF_BLOCK), col), pl.BlockSpec((1, FF_BLOCK), col)] + c_in,
        out_specs=[pl.BlockSpec((tm, FF_BLOCK), lambda i, j: (i, j))] + c_out,
        out_shape=[jax.ShapeDtypeStruct((m, d_ff), BF16)] + c_shapes,
        scratch_shapes=[pltpu.VMEM((2, tm + CARRY_ROWS, FF_BLOCK), F32),
                        pltpu.VMEM((nj, 2, CARRY_ROWS, FF_BLOCK), F32)],
        compiler_params=_params("arbitrary", "arbitrary"),
        name="ffn_up_conv_gate",
    )(x, w_gate, w_up, cw_gate, cw_up, cb_gate, cb_up, *[job[0] for job in cast_jobs])
    return res[0], res[1:]


def _proj_ln_kernel(*refs, nk, k_tail, n_chunks, n_jobs):
    a_ref, w_ref, x_ref, g_ref, b_ref = refs[:5]
    of_ref, ob_ref = refs[5 + n_jobs:7 + n_jobs]
    mu_ref, rstd_ref = refs[-2:]
    _run_side_casts(refs[5:5 + n_jobs], refs[7 + n_jobs:-2])
    k = pl.program_id(1)
    tm, n = of_ref.shape
    cw = n // n_chunks

    def add_residual(c):
        of_ref[:, c * cw:(c + 1) * cw] += DEEPNORM_ALPHA * x_ref[...]

    @pl.when(k == 0)
    def _():
        of_ref[...] = jnp.dot(a_ref[...], w_ref[...], preferred_element_type=F32)
        add_residual(0)

    for c in range(1, n_chunks):
        pl.when(k == c)(functools.partial(add_residual, c))

    @pl.when((k > 0) & (k < nk - 1))
    def _():
        of_ref[...] += jnp.dot(a_ref[...], w_ref[...], preferred_element_type=F32)

    @pl.when(k == nk - 1)
    def _():
        z = of_ref[...] + jnp.dot(a_ref[:, :k_tail], w_ref[:k_tail, :], preferred_element_type=F32)
        of_ref[...] = z
        d = z - z[:, 0:1]
        s1 = jnp.mean(d, axis=-1, keepdims=True)
        var = jnp.maximum(jnp.mean(d * d, axis=-1, keepdims=True) - s1 * s1, 0.0)
        mu_ref[...] = jnp.broadcast_to(z[:, 0:1] + s1, (tm, LANES))
        rstd_ref[...] = jnp.broadcast_to(lax.rsqrt(var + LN_EPS), (tm, LANES))

        def rows_at(r):
            return pl.ds(pl.multiple_of(r * LN_ROWS, LN_ROWS), LN_ROWS)

        def across(stat_ref, r):
            return jnp.concatenate([stat_ref[rows_at(r), :]] * (n // LANES), axis=1)

        def norm_pass(r, carry):
            d = of_ref[rows_at(r), :] - across(mu_ref, r)
            y = d * across(rstd_ref, r) * g_ref[...] + b_ref[...]
            of_ref[rows_at(r), :] = y
            ob_ref[rows_at(r), :] = y.astype(ob_ref.dtype)
            return carry

        lax.fori_loop(0, tm // LN_ROWS, norm_pass, 0, unroll=4)


def _proj_ln(a, w, x, g, b, *, name, cast_jobs=(), tk=1024):
    m, kdim = a.shape
    n = w.shape[1]
    nk = pl.cdiv(kdim, tk)
    k_tail = kdim - (nk - 1) * tk
    n_chunks = 1
    while n_chunks * 2 <= min(nk, 8):
        n_chunks *= 2
    cw = n // n_chunks
    tm = 1024 if n_chunks >= 4 else 512
    out_mode = pl.Buffered(1 if tm == 1024 else 2)
    out_b_mode = pl.Buffered(1 if tm == 1024 and n_chunks < 8 else 2)
    c_in, c_out, c_shapes = _side_cast_plumbing(cast_jobs, (m // tm) * nk, lambda i, k: i * nk + k)
    res = pl.pallas_call(
        functools.partial(_proj_ln_kernel, nk=nk, k_tail=k_tail, n_chunks=n_chunks, n_jobs=len(cast_jobs)),
        grid=(m // tm, nk),
        in_specs=[pl.BlockSpec((tm, tk), lambda i, k: (i, k)),
                  pl.BlockSpec((tk, n), lambda i, k: (k, 0)),
                  pl.BlockSpec((tm, cw), lambda i, k: (i, jnp.minimum(k, n_chunks - 1))),
                  pl.BlockSpec((LN_ROWS, n), lambda i, k: (0, 0)),
                  pl.BlockSpec((LN_ROWS, n), lambda i, k: (0, 0))] + c_in,
        out_specs=[pl.BlockSpec((tm, n), lambda i, k: (i, 0), pipeline_mode=out_mode),
                   pl.BlockSpec((tm, n), lambda i, k: (i, 0), pipeline_mode=out_b_mode)] + c_out,
        out_shape=[jax.ShapeDtypeStruct((m, n), F32), jax.ShapeDtypeStruct((m, n), BF16)] + c_shapes,
        scratch_shapes=[pltpu.VMEM((tm, LANES), F32), pltpu.VMEM((tm, LANES), F32)],
        compiler_params=_params("arbitrary", "arbitrary"),
        name=name,
    )(a, w, x, jnp.broadcast_to(g.reshape(1, n), (LN_ROWS, n)), jnp.broadcast_to(b.reshape(1, n), (LN_ROWS, n)),
      *[job[0] for job in cast_jobs])
    return res[0], res[1], res[2:]


def _rotary_inv_freq():
    half = B_ROT_DIM // 2
    inv_freq = jnp.float32(ROPE_THETA) ** (-jnp.arange(half, dtype=F32) * 2.0 / B_ROT_DIM)
    zeros = jnp.zeros((B_HEAD_DIM - B_ROT_DIM,), F32)
    return jnp.concatenate([-inv_freq, inv_freq, zeros]).reshape(1, B_HEAD_DIM)


def kernel(x, positions, a_w_in, a_ln_g, a_ln_b, a_w_s, a_b_s, a_w_out, b_w_in, b_w_out, c_w_in, c_w_group, c_scale, c_w_out, f_w_up, f_conv_w, f_conv_b, f_w_down, ln_mix_g, ln_mix_b, ln_ffn_g, ln_ffn_b):
    batch, seq, d = x.shape
    m = batch * seq
    xf = x.reshape(m, d)
    xb = xf.astype(BF16)
    posb = jnp.broadcast_to(positions.reshape(m, 1).astype(F32), (m, B_HEAD_DIM))
    invf = _rotary_inv_freq()

    c_w_group_2d = c_w_group.reshape(c_w_group.shape[0], -1, c_w_group.shape[-1])
    a_in = [a_w_in[0].astype(BF16), None]
    a_out = [None, None]
    ffn_up_w = None
    for layer in range(DEPTH):
        kind, idx = layer % N_MIXERS, layer // N_MIXERS
        if kind == 0:
            jobs = [(f_w_up, layer, True)] + ([(a_w_out, 0, False)] if layer == 0 else [])
            h, casts = _mm(xb, a_in[idx], epilogue=_gelu, out_dtype=BF16, name="gmlp_in_gelu", cast_jobs=jobs)
            ffn_up_w = casts[:2]
            if layer == 0:
                a_out[0] = casts[2]
            bs_exp = jnp.repeat(a_b_s[idx].T, A_GROUP_DIM, axis=1)
            y = _spatial_gate(h, a_ln_g[idx].reshape(1, d), a_ln_b[idx].reshape(1, d), a_w_s[idx], bs_exp)
            w_out = a_out[idx]
        elif kind == 1:
            outs, lses, dilations = [], [], []
            for group, (window, dilation) in enumerate(B_PATTERNS):
                assert window // dilation == B_QBLOCK
                qkv, _ = _qkv_proj(xb, b_in, group, posb, invf, dilation)
                jobs = [(f_w_up, layer, True)] if group == 0 else []
                o, lse, casts = _dilated_attention(qkv, seq, dilation, cast_jobs=jobs)
                if group == 0:
                    ffn_up_w = casts
                outs.append(o)
                lses.append(lse)
                dilations.append(dilation)
            y = _combine_groups(outs, lses, tuple(dilations), m)
            w_out = b_out
        else:
            h, ffn_up_w = _mm(xb, c_in, epilogue=lambda t: t, out_dtype=F32, name="pool_in",
                              cast_jobs=[(f_w_up, layer, True)])
            y = _pool_mix(h, c_group.reshape(c_w_group[idx].shape), c_scale[idx].reshape(1, d), seq)
            w_out = c_out
        xf, xb, _ = _proj_ln(y, w_out, xf, ln_mix_g[layer], ln_mix_b[layer], name="mixer_out_ln")

        jobs = [(f_w_down, layer, False)]
        nxt = layer + 1
        nxt_kind, nxt_idx = nxt % N_MIXERS, nxt // N_MIXERS
        if nxt < DEPTH and nxt_kind == 0:
            jobs += [(a_w_in, nxt_idx, False), (a_w_out, nxt_idx, False)]
        elif nxt < DEPTH and nxt_kind == 1:
            jobs += [(b_w_in, nxt_idx, False), (b_w_out, nxt_idx, False)]
        elif nxt < DEPTH:
            jobs += [(c_w_in, nxt_idx, False), (c_w_out, nxt_idx, False), (c_w_group_2d, nxt_idx, False)]
        cw, cb = f_conv_w[layer], f_conv_b[layer].reshape(1, -1)
        g, casts = _ffn_up(xb, ffn_up_w[0], ffn_up_w[1], cw[:, :D_FF], cw[:, D_FF:], cb[:, :D_FF], cb[:, D_FF:],
                           seq, cast_jobs=jobs)
        w_down = casts[0]
        if nxt < DEPTH and nxt_kind == 0:
            a_in[nxt_idx], a_out[nxt_idx] = casts[1:3]
        elif nxt < DEPTH and nxt_kind == 1:
            b_in, b_out = casts[1:3]
        elif nxt < DEPTH:
            c_in, c_out, c_group = casts[1:4]
        xf, xb, _ = _proj_ln(g, w_down, xf, ln_ffn_g[layer], ln_ffn_b[layer], name="ffn_down_ln")

    return xf.reshape(batch, seq, d)
```

```python
import functools
import math

import jax
import jax.numpy as jnp
from jax import lax
from jax.experimental import pallas as pl
from jax.experimental.pallas import tpu as pltpu

F32 = jnp.float32
BF16 = jnp.bfloat16

D_MODEL = 4096
DEPTH = 4
N_MIXERS = 3

A_CHUNK = 128
A_GROUPS = 16
A_GROUP_DIM = D_MODEL // A_GROUPS

B_PATTERNS = ((128, 1), (512, 4), (2048, 16))
B_HEADS = 16
B_HEAD_DIM = 128
B_ROT_DIM = B_HEAD_DIM // 4
B_QBLOCK = 128
ATTN_PAIR = 2
B_WIDTH = B_HEADS * B_HEAD_DIM
ROPE_THETA = 500000.0

C_WINDOWS = (2, 4, 8, 16)
C_GROUP_DIM = D_MODEL // len(C_WINDOWS)
C_HALO = 16

D_FF = 11008
FF_BLOCK = 768
CONV_WIDTH = 3
CARRY_ROWS = 8

LN_EPS = 1e-5
LN_ROWS = 8
DEEPNORM_ALPHA = (2 * DEPTH) ** 0.25

LANES = 128
SIDE_CAST_WINDOW_BUDGET_BYTES = 12 * 1024 * 1024
V7X_VMEM_LIMIT_BYTES = 63 * 1024 * 1024


def _params(*semantics, flags=None):
    return pltpu.CompilerParams(dimension_semantics=semantics, vmem_limit_bytes=V7X_VMEM_LIMIT_BYTES,
                                flags=flags)


def _layer_norm(z, g, b):
    mu = jnp.mean(z, axis=-1, keepdims=True)
    d = z - mu
    var = jnp.mean(d * d, axis=-1, keepdims=True)
    return d * lax.rsqrt(var + LN_EPS) * g + b


def _cast_slab_rows(total_rows, n_steps, smallest=16):
    rows = smallest
    while total_rows % rows or total_rows // rows > n_steps:
        rows *= 2
        assert rows <= total_rows
    return rows


def _side_cast_plumbing(jobs, n_steps, step_of):
    in_specs, out_specs, out_shapes = [], [], []
    for src, layer, split in jobs:
        _, rows, width = src.shape
        slab = _cast_slab_rows(rows, n_steps)
        slab_of = lambda *g, last=rows // slab - 1: jnp.minimum(step_of(*g), last)
        in_specs.append(pl.BlockSpec((None, slab, width), lambda *g, f=slab_of, l=layer: (l, f(*g), 0)))
        parts = 2 if split else 1
        out_specs += [pl.BlockSpec((slab, width // parts), lambda *g, f=slab_of: (f(*g), 0))] * parts
        out_shapes += [jax.ShapeDtypeStruct((rows, width // parts), BF16)] * parts
    return in_specs, out_specs, out_shapes


def _host_column_block(jobs, row_tiles, n, tn):
    def window_bytes(tn):
        return sum(2 * _cast_slab_rows(src.shape[1], row_tiles * (n // tn)) * src.shape[2] * (4 + 2)
                   for src, _, _ in jobs)
    while window_bytes(tn) > SIDE_CAST_WINDOW_BUDGET_BYTES and tn > 2 * LANES:
        tn //= 2
    return tn


def _run_side_casts(src_refs, dst_refs):
    dst = list(dst_refs)
    for src in src_refs:
        width = dst[0].shape[1]
        for part in range(src.shape[1] // width):
            dst.pop(0)[...] = src[:, part * width:(part + 1) * width].astype(BF16)
    assert not dst


def _gelu(x):
    return 0.5 * x * (1.0 + lax.erf(x * (1.0 / math.sqrt(2.0))))


def _mm_kernel(*refs, epilogue, n_jobs):
    x_ref, w_ref = refs[:2]
    o_ref = refs[2 + n_jobs]
    _run_side_casts(refs[2:2 + n_jobs], refs[3 + n_jobs:])
    acc = jnp.dot(x_ref[...], w_ref[...], preferred_element_type=F32)
    o_ref[...] = epilogue(acc).astype(o_ref.dtype)


def _mm(x, w, *, epilogue, out_dtype, name, cast_jobs=(), tm=1024, tn=1024):
    m, k = x.shape
    n = w.shape[1]
    tn = _host_column_block(cast_jobs, m // tm, n, tn)
    nj = n // tn
    c_in, c_out, c_shapes = _side_cast_plumbing(cast_jobs, (m // tm) * nj, lambda i, j: i * nj + j)
    res = pl.pallas_call(
        functools.partial(_mm_kernel, epilogue=epilogue, n_jobs=len(cast_jobs)),
        grid=(m // tm, nj),
        in_specs=[pl.BlockSpec((tm, k), lambda i, j: (i, 0)),
                  pl.BlockSpec((k, tn), lambda i, j: (0, j))] + c_in,
        out_specs=[pl.BlockSpec((tm, tn), lambda i, j: (i, j))] + c_out,
        out_shape=[jax.ShapeDtypeStruct((m, n), out_dtype)] + c_shapes,
        compiler_params=_params("arbitrary", "arbitrary"),
        name=name,
    )(x, w, *[job[0] for job in cast_jobs])
    return res[0], res[1:]


def _qkv_kernel(*refs, tm, tn, dilation, n_jobs):
    x_ref, w_ref, pos_ref, invf_ref = refs[:4]
    o_ref = refs[4 + n_jobs]
    n_dst = len(refs) - (7 + n_jobs) - (dilation > 1)
    cos_ref, sin_ref, *perm_scratch = refs[5 + n_jobs + n_dst:]
    _run_side_casts(refs[4:4 + n_jobs], refs[5 + n_jobs:5 + n_jobs + n_dst])
    j = pl.program_id(1)

    @pl.when(j == 0)
    def _():
        ang = pos_ref[...] * invf_ref[...]
        cos_ref[...] = jnp.cos(ang)
        sin_ref[...] = jnp.sin(ang)

    def block(rotate):
        out = jnp.dot(x_ref[...], w_ref[...], preferred_element_type=F32)
        if rotate:
            reps = tn // B_HEAD_DIM
            cos = jnp.concatenate([cos_ref[...]] * reps, axis=1)
            sin = jnp.concatenate([sin_ref[...]] * reps, axis=1)
            lane = lax.broadcasted_iota(jnp.int32, out.shape, 1) & (B_HEAD_DIM - 1)
            half = B_ROT_DIM // 2
            partner = jnp.where(lane < half,
                                pltpu.roll(out, tn - half, 1),
                                pltpu.roll(out, half, 1))
            out = out * cos + partner * sin

        chunk = B_QBLOCK * dilation
        if dilation == 1:
            for c in range(tm // chunk):
                o_ref[c, 0] = out[c * chunk:(c + 1) * chunk].astype(o_ref.dtype)
            return
        (buf,) = perm_scratch
        for t in range(tn // LANES):
            cols = slice(t * LANES, (t + 1) * LANES)
            buf[t] = out[:, cols]
            if chunk <= tm:
                for c in range(tm // chunk):
                    for r in range(dilation):
                        rows = pl.ds(c * chunk + r, B_QBLOCK, stride=dilation)
                        o_ref[c, r, :, cols] = buf[t, rows, :].astype(o_ref.dtype)
            else:
                for r in range(dilation):
                    rows = pl.ds(r, tm // dilation, stride=dilation)
                    o_ref[r, :, cols] = buf[t, rows, :].astype(o_ref.dtype)

    is_v = (j * tn) // B_WIDTH == 2
    pl.when(jnp.logical_not(is_v))(lambda: block(True))
    pl.when(is_v)(lambda: block(False))


def _qkv_proj(x, w, group, posb, invf, dilation, *, cast_jobs=(), tm=1024, tn=1024):
    m, k = x.shape
    n = 3 * B_WIDTH
    tn = _host_column_block(cast_jobs, m // tm, n, tn)
    nj = n // tn
    first = group * nj
    c_in, c_out, c_shapes = _side_cast_plumbing(cast_jobs, (m // tm) * nj, lambda i, j: i * nj + j)
    chunk = B_QBLOCK * dilation
    if chunk <= tm:
        out_spec = pl.BlockSpec((tm // chunk, dilation, B_QBLOCK, tn), lambda i, j: (i, 0, 0, j))
    else:
        tiles_per_chunk = chunk // tm
        out_spec = pl.BlockSpec((None, dilation, tm // dilation, tn),
                                lambda i, j: (i // tiles_per_chunk, 0, i % tiles_per_chunk, j))
    scratch = [pltpu.VMEM((tm, B_HEAD_DIM), F32), pltpu.VMEM((tm, B_HEAD_DIM), F32)]
    if dilation > 1:
        scratch.append(pltpu.VMEM((tn // LANES, tm, LANES), F32))
    res = pl.pallas_call(
        functools.partial(_qkv_kernel, tm=tm, tn=tn, dilation=dilation, n_jobs=len(cast_jobs)),
        grid=(m // tm, nj),
        in_specs=[pl.BlockSpec((tm, k), lambda i, j: (i, 0)),
                  pl.BlockSpec((k, tn), lambda i, j: (0, first + j)),
                  pl.BlockSpec((tm, B_HEAD_DIM), lambda i, j: (i, 0)),
                  pl.BlockSpec((1, B_HEAD_DIM), lambda i, j: (0, 0))] + c_in,
        out_specs=[out_spec] + c_out,
        out_shape=[jax.ShapeDtypeStruct((m // chunk, dilation, B_QBLOCK, n), BF16)] + c_shapes,
        scratch_shapes=scratch,
        compiler_params=_params("arbitrary", "arbitrary"),
        name=f"qkv_proj_d{dilation}",
    )(x, w, posb, invf, *[job[0] for job in cast_jobs])
    return res[0], res[1:]


def _attn_kernel(*refs, chunks_per_seq, n_jobs):
    q_ref, kp_ref, kc_ref, vp_ref, vc_ref = refs[:5]
    o_ref, l_ref = refs[5 + n_jobs:7 + n_jobs]
    _run_side_casts(refs[5:5 + n_jobs], refs[7 + n_jobs:])
    c = pl.program_id(0)
    q_blk = B_QBLOCK
    qi = lax.broadcasted_iota(jnp.int32, (q_blk, 2 * q_blk), 0)
    kj = lax.broadcasted_iota(jnp.int32, (q_blk, 2 * q_blk), 1)
    band = (kj >= qi) & (kj <= qi + q_blk)
    first_has_prev = ((ATTN_PAIR * c) % chunks_per_seq) > 0
    masks = (band & ((kj >= q_blk) | first_has_prev), band)
    scale = B_HEAD_DIM ** -0.5
    head_lane = lax.broadcasted_iota(jnp.int32, (q_blk, LANES), 1)
    for blk in range(ATTN_PAIR):
        lse_tile = jnp.zeros((q_blk, LANES), F32)
        for h in range(B_HEADS):
            sl = slice(h * B_HEAD_DIM, (h + 1) * B_HEAD_DIM)
            q = q_ref[blk, :, sl]
            k_prev = kp_ref[:, sl] if blk == 0 else kc_ref[blk - 1, :, sl]
            v_prev = vp_ref[:, sl] if blk == 0 else vc_ref[blk - 1, :, sl]
            k2 = jnp.concatenate([k_prev, kc_ref[blk, :, sl]], axis=0)
            v2 = jnp.concatenate([v_prev, vc_ref[blk, :, sl]], axis=0)
            s = lax.dot_general(q, k2, (((1,), (1,)), ((), ())), preferred_element_type=F32) * scale
            s = jnp.where(masks[blk], s, -jnp.inf)
            mx = jnp.max(s, axis=-1, keepdims=True)
            p = jnp.exp(s - mx)
            den = jnp.sum(p, axis=-1, keepdims=True)
            o = jnp.dot(p.astype(BF16), v2, preferred_element_type=F32) / den
            o_ref[blk, :, sl] = o.astype(o_ref.dtype)
            lse_tile = jnp.where(head_lane == h, mx + jnp.log(den), lse_tile)
        l_ref[blk] = lse_tile


def _dilated_attention(qkv, seq, dilation, *, cast_jobs=()):
    n_chunks = qkv.shape[0]
    chunks_per_seq = seq // (B_QBLOCK * dilation)
    assert chunks_per_seq % ATTN_PAIR == 0
    n_pairs = n_chunks // ATTN_PAIR

    def cur(which):
        return pl.BlockSpec((ATTN_PAIR, None, B_QBLOCK, B_WIDTH), lambda c, r: (c, r, 0, which))

    def prev(which):
        return pl.BlockSpec((None, None, B_QBLOCK, B_WIDTH),
                            lambda c, r: (jnp.maximum(ATTN_PAIR * c - 1, 0), r, 0, which))

    c_in, c_out, c_shapes = _side_cast_plumbing(cast_jobs, n_pairs * dilation, lambda c, r: c * dilation + r)
    res = pl.pallas_call(
        functools.partial(_attn_kernel, chunks_per_seq=chunks_per_seq, n_jobs=len(cast_jobs)),
        grid=(n_pairs, dilation),
        in_specs=[cur(0), prev(1), cur(1), prev(2), cur(2)] + c_in,
        out_specs=[pl.BlockSpec((ATTN_PAIR, None, B_QBLOCK, B_WIDTH), lambda c, r: (c, r, 0, 0)),
                   pl.BlockSpec((ATTN_PAIR, None, B_QBLOCK, LANES), lambda c, r: (c, r, 0, 0))] + c_out,
        out_shape=[jax.ShapeDtypeStruct((n_chunks, dilation, B_QBLOCK, B_WIDTH), BF16),
                   jax.ShapeDtypeStruct((n_chunks, dilation, B_QBLOCK, LANES), F32)] + c_shapes,
        compiler_params=_params("arbitrary", "arbitrary"),
        name=f"dilated_attn_d{dilation}",
    )(qkv, qkv, qkv, qkv, qkv, *[job[0] for job in cast_jobs])
    return res[0], res[1], res[2:]


def _combine_kernel(*refs, tm, dilations):
    n = len(dilations)
    o_refs, l_refs, out_ref = refs[:n], refs[n:2 * n], refs[2 * n]
    o_bufs, l_bufs = refs[2 * n + 1:3 * n + 1], refs[3 * n + 1:4 * n + 1]

    def scatter(src_o, src_l, o_buf, l_buf, rows):
        l_buf[rows, :] = src_l
        for h in range(B_HEADS):
            o_buf[h, rows, :] = src_o[:, h * B_HEAD_DIM:(h + 1) * B_HEAD_DIM].astype(F32)

    for o_ref, l_ref, o_buf, l_buf, d in zip(o_refs, l_refs, o_bufs, l_bufs, dilations):
        chunk = B_QBLOCK * d
        if chunk <= tm:
            for c in range(tm // chunk):
                for r in range(d):
                    rows = pl.ds(c * chunk + r, B_QBLOCK, stride=d) if d > 1 else pl.ds(c * chunk, B_QBLOCK)
                    scatter(o_ref[c, r], l_ref[c, r], o_buf, l_buf, rows)
        else:
            for r in range(d):
                scatter(o_ref[r], l_ref[r], o_buf, l_buf, pl.ds(r, tm // d, stride=d))

    lses = [l_buf[...] for l_buf in l_bufs]
    mx = functools.reduce(jnp.maximum, lses)
    es = [jnp.exp(l - mx) for l in lses]
    tot = functools.reduce(lambda a, b: a + b, es)
    ws = [e / tot for e in es]
    for h in range(B_HEADS):
        acc = None
        for w, o_buf in zip(ws, o_bufs):
            term = jnp.broadcast_to(w[:, h:h + 1], (tm, B_HEAD_DIM)) * o_buf[h]
            acc = term if acc is None else acc + term
        out_ref[:, h * B_HEAD_DIM:(h + 1) * B_HEAD_DIM] = acc.astype(out_ref.dtype)


def _combine_groups(outs, lses, dilations, m, *, tm=512):
    def spec(d, width):
        chunk = B_QBLOCK * d
        if chunk <= tm:
            return pl.BlockSpec((tm // chunk, d, B_QBLOCK, width), lambda i: (i, 0, 0, 0))
        tiles_per_chunk = chunk // tm
        return pl.BlockSpec((None, d, tm // d, width),
                            lambda i: (i // tiles_per_chunk, 0, i % tiles_per_chunk, 0))

    n = len(dilations)
    return pl.pallas_call(
        functools.partial(_combine_kernel, tm=tm, dilations=dilations),
        grid=(m // tm,),
        in_specs=[spec(d, B_WIDTH) for d in dilations] + [spec(d, LANES) for d in dilations],
        out_specs=pl.BlockSpec((tm, B_WIDTH), lambda i: (i, 0)),
        out_shape=jax.ShapeDtypeStruct((m, B_WIDTH), BF16),
        scratch_shapes=[pltpu.VMEM((B_HEADS, tm, B_HEAD_DIM), F32)] * n + [pltpu.VMEM((tm, LANES), F32)] * n,
        compiler_params=_params("parallel"),
        name="attn_combine",
    )(*outs, *lses)


def _gate_kernel(u_ref, v_ref, g_ref, b_ref, ws_ref, bs_ref, o_ref, *, rows):
    vn = _layer_norm(v_ref[...].astype(F32), g_ref[...], b_ref[...]).astype(BF16)
    ti = lax.broadcasted_iota(jnp.int32, (A_CHUNK, A_CHUNK), 0)
    si = lax.broadcasted_iota(jnp.int32, (A_CHUNK, A_CHUNK), 1)
    causal = si <= ti
    for g in range(A_GROUPS):
        cols = slice(g * A_GROUP_DIM, (g + 1) * A_GROUP_DIM)
        w = jnp.where(causal, ws_ref[g], 0.0).astype(BF16)
        bias = bs_ref[:, cols]
        for c in range(rows // A_CHUNK):
            rws = slice(c * A_CHUNK, (c + 1) * A_CHUNK)
            mixed = jnp.dot(w, vn[rws, cols], preferred_element_type=F32) + bias
            o_ref[rws, cols] = (u_ref[rws, cols].astype(F32) * mixed).astype(o_ref.dtype)


def _spatial_gate(h, ln_g, ln_b, w_s, bs_exp, *, rows=256):
    m = h.shape[0]
    width = D_MODEL
    return pl.pallas_call(
        functools.partial(_gate_kernel, rows=rows),
        grid=(m // rows,),
        in_specs=[pl.BlockSpec((rows, width), lambda i: (i, 0)),
                  pl.BlockSpec((rows, width), lambda i: (i, 1)),
                  pl.BlockSpec((1, width), lambda i: (0, 0)),
                  pl.BlockSpec((1, width), lambda i: (0, 0)),
                  pl.BlockSpec((A_GROUPS, A_CHUNK, A_CHUNK), lambda i: (0, 0, 0)),
                  pl.BlockSpec((A_CHUNK, width), lambda i: (0, 0))],
        out_specs=pl.BlockSpec((rows, width), lambda i: (i, 0)),
        out_shape=jax.ShapeDtypeStruct((m, width), BF16),
        compiler_params=_params("parallel"),
        name="spatial_gate",
    )(h, h, ln_g, ln_b, w_s, bs_exp)


def _pool_kernel(h_ref, halo_ref, wg_ref, scale_ref, o_ref, *, rows, seq):
    i = pl.program_id(0)
    t0 = (i * rows) % seq
    pos = t0 + lax.broadcasted_iota(jnp.int32, (rows, 1), 0)
    keep_halo = t0 > 0
    for g, window in enumerate(C_WINDOWS):
        cols = slice(g * C_GROUP_DIM, (g + 1) * C_GROUP_DIM)
        hg = h_ref[:, cols]
        halo = jnp.where(keep_halo, halo_ref[:, cols], 0.0)
        acc = jnp.concatenate([halo, hg], axis=0)
        span = 1
        while span < window:
            acc = acc + pltpu.roll(acc, span, 0)
            span *= 2
        den = jnp.minimum(pos + 1, window).astype(F32)
        pooled = acc[C_HALO:, :] / den - hg
        y = jnp.dot(pooled.astype(BF16), wg_ref[g], preferred_element_type=F32) * scale_ref[:, cols]
        o_ref[:, cols] = y.astype(o_ref.dtype)


def _pool_mix(h, w_group, scale, seq, *, rows=256):
    m, width = h.shape
    halo_per_block = rows // C_HALO
    return pl.pallas_call(
        functools.partial(_pool_kernel, rows=rows, seq=seq),
        grid=(m // rows,),
        in_specs=[pl.BlockSpec((rows, width), lambda i: (i, 0)),
                  pl.BlockSpec((C_HALO, width), lambda i: (jnp.maximum(i * halo_per_block - 1, 0), 0)),
                  pl.BlockSpec((len(C_WINDOWS), C_GROUP_DIM, C_GROUP_DIM), lambda i: (0, 0, 0)),
                  pl.BlockSpec((1, width), lambda i: (0, 0))],
        out_specs=pl.BlockSpec((rows, width), lambda i: (i, 0)),
        out_shape=jax.ShapeDtypeStruct((m, width), BF16),
        compiler_params=_params("parallel"),
        name="pool_mix",
    )(h, h, w_group, scale)


def _ffn_up_kernel(*refs, tm, nj, tail_width, tiles_per_seq, n_jobs):
    x_ref, wg_ref, wu_ref, cwg_ref, cwu_ref, cbg_ref, cbu_ref = refs[:7]
    o_ref = refs[7 + n_jobs]
    hbuf, carry = refs[-2:]
    i = pl.program_id(0)
    j = pl.program_id(1)
    seq_start = (i % tiles_per_seq) == 0

    @pl.when((i == 0) & (j == 0))
    def _():
        carry[...] = jnp.zeros(carry.shape, F32)

    @pl.when(seq_start)
    def _():
        hbuf[:, 0:CARRY_ROWS, :] = jnp.zeros((2, CARRY_ROWS, FF_BLOCK), F32)

    @pl.when(jnp.logical_not(seq_start))
    def _():
        hbuf[:, 0:CARRY_ROWS, :] = carry[j]

    _run_side_casts(refs[7:7 + n_jobs], refs[8 + n_jobs:-2])

    def block(width):
        x = x_ref[...]
        halves = []
        for s, (w_ref, cw_ref, cb_ref) in enumerate(((wg_ref, cwg_ref, cbg_ref), (wu_ref, cwu_ref, cbu_ref))):
            h = jnp.dot(x, w_ref[:, :width], preferred_element_type=F32)
            hbuf[s, CARRY_ROWS:, :width] = h
            carry[j, s, :, :width] = h[tm - CARRY_ROWS:, :]
            h1 = hbuf[s, CARRY_ROWS - 1:CARRY_ROWS - 1 + tm, :width]
            h2 = hbuf[s, CARRY_ROWS - 2:CARRY_ROWS - 2 + tm, :width]
            halves.append(cw_ref[2:3, :width] * h + cw_ref[1:2, :width] * h1 + cw_ref[0:1, :width] * h2
                          + cb_ref[:, :width])
        gate, up = halves
        o_ref[:, :width] = (gate / (1.0 + jnp.exp(-gate)) * up).astype(o_ref.dtype)

    if tail_width == FF_BLOCK:
        block(FF_BLOCK)
    else:
        pl.when(j < nj - 1)(lambda: block(FF_BLOCK))
        pl.when(j == nj - 1)(lambda: block(tail_width))


def _ffn_up(x, w_gate, w_up, cw_gate, cw_up, cb_gate, cb_up, seq, *, cast_jobs=(), tm=1024):
    m, k = x.shape
    d_ff = w_gate.shape[1]
    nj = pl.cdiv(d_ff, FF_BLOCK)
    col = lambda i, j: (0, j)
    c_in, c_out, c_shapes = _side_cast_plumbing(cast_jobs, (m // tm) * nj, lambda i, j: i * nj + j)
    res = pl.pallas_call(
        functools.partial(_ffn_up_kernel, tm=tm, nj=nj, tail_width=d_ff - (nj - 1) * FF_BLOCK,
                          tiles_per_seq=seq // tm, n_jobs=len(cast_jobs)),
        grid=(m // tm, nj),
        in_specs=[pl.BlockSpec((tm, k), lambda i, j: (i, 0), pipeline_mode=pl.Buffered(1)),
                  pl.BlockSpec((k, FF_BLOCK), col), pl.BlockSpec((k, FF_BLOCK), col),
                  pl.BlockSpec((CONV_WIDTH, FF_BLOCK), col), pl.BlockSpec((CONV_WIDTH, FF_BLOCK), col),
                  pl.BlockSpec((1, FF_BLOCK), col), pl.BlockSpec((1, FF_BLOCK), col)] + c_in,
        out_specs=[pl.BlockSpec((tm, FF_BLOCK), lambda i, j: (i, j))] + c_out,
        out_shape=[jax.ShapeDtypeStruct((m, d_ff), BF16)] + c_shapes,
        scratch_shapes=[pltpu.VMEM((2, tm + CARRY_ROWS, FF_BLOCK), F32),
                        pltpu.VMEM((nj, 2, CARRY_ROWS, FF_BLOCK), F32)],
        compiler_params=_params("arbitrary", "arbitrary"),
        name="ffn_up_conv_gate",
    )(x, w_gate, w_up, cw_gate, cw_up, cb_gate, cb_up, *[job[0] for job in cast_jobs])
    return res[0], res[1:]


def _proj_ln_kernel(*refs, nk, k_tail, n_chunks, n_jobs):
    a_ref, w_ref, x_ref, g_ref, b_ref = refs[:5]
    of_ref, ob_ref = refs[5 + n_jobs:7 + n_jobs]
    mu_ref, rstd_ref = refs[-2:]
    _run_side_casts(refs[5:5 + n_jobs], refs[7 + n_jobs:-2])
    k = pl.program_id(1)
    tm, n = of_ref.shape
    cw = n // n_chunks

    def add_residual(c):
        of_ref[:, c * cw:(c + 1) * cw] += DEEPNORM_ALPHA * x_ref[...]

    @pl.when(k == 0)
    def _():
        of_ref[...] = jnp.dot(a_ref[...], w_ref[...], preferred_element_type=F32)
        add_residual(0)

    for c in range(1, n_chunks):
        pl.when(k == c)(functools.partial(add_residual, c))

    @pl.when((k > 0) & (k < nk - 1))
    def _():
        of_ref[...] += jnp.dot(a_ref[...], w_ref[...], preferred_element_type=F32)

    @pl.when(k == nk - 1)
    def _():
        z = of_ref[...] + jnp.dot(a_ref[:, :k_tail], w_ref[:k_tail, :], preferred_element_type=F32)
        of_ref[...] = z
        d = z - z[:, 0:1]
        s1 = jnp.mean(d, axis=-1, keepdims=True)
        var = jnp.maximum(jnp.mean(d * d, axis=-1, keepdims=True) - s1 * s1, 0.0)
        mu_ref[...] = jnp.broadcast_to(z[:, 0:1] + s1, (tm, LANES))
        rstd_ref[...] = jnp.broadcast_to(lax.rsqrt(var + LN_EPS), (tm, LANES))

        def rows_at(r):
            return pl.ds(pl.multiple_of(r * LN_ROWS, LN_ROWS), LN_ROWS)

        def across(stat_ref, r):
            return jnp.concatenate([stat_ref[rows_at(r), :]] * (n // LANES), axis=1)

        def norm_pass(r, carry):
            d = of_ref[rows_at(r), :] - across(mu_ref, r)
            y = d * across(rstd_ref, r) * g_ref[...] + b_ref[...]
            of_ref[rows_at(r), :] = y
            ob_ref[rows_at(r), :] = y.astype(ob_ref.dtype)
            return carry

        lax.fori_loop(0, tm // LN_ROWS, norm_pass, 0, unroll=4)


def _proj_ln(a, w, x, g, b, *, name, cast_jobs=(), tk=1024):
    m, kdim = a.shape
    n = w.shape[1]
    nk = pl.cdiv(kdim, tk)
    k_tail = kdim - (nk - 1) * tk
    n_chunks = 1
    while n_chunks * 2 <= min(nk, 8):
        n_chunks *= 2
    cw = n // n_chunks
    tm = 1024 if n_chunks >= 4 else 512
    out_mode = pl.Buffered(1 if tm == 1024 else 2)
    out_b_mode = pl.Buffered(1 if tm == 1024 and n_chunks < 8 else 2)
    c_in, c_out, c_shapes = _side_cast_plumbing(cast_jobs, (m // tm) * nk, lambda i, k: i * nk + k)
    res = pl.pallas_call(
        functools.partial(_proj_ln_kernel, nk=nk, k_tail=k_tail, n_chunks=n_chunks, n_jobs=len(cast_jobs)),
        grid=(m // tm, nk),
        in_specs=[pl.BlockSpec((tm, tk), lambda i, k: (i, k)),
                  pl.BlockSpec((tk, n), lambda i, k: (k, 0)),
                  pl.BlockSpec((tm, cw), lambda i, k: (i, jnp.minimum(k, n_chunks - 1))),
                  pl.BlockSpec((LN_ROWS, n), lambda i, k: (0, 0)),
                  pl.BlockSpec((LN_ROWS, n), lambda i, k: (0, 0))] + c_in,
        out_specs=[pl.BlockSpec((tm, n), lambda i, k: (i, 0), pipeline_mode=out_mode),
                   pl.BlockSpec((tm, n), lambda i, k: (i, 0), pipeline_mode=out_b_mode)] + c_out,
        out_shape=[jax.ShapeDtypeStruct((m, n), F32), jax.ShapeDtypeStruct((m, n), BF16)] + c_shapes,
        scratch_shapes=[pltpu.VMEM((tm, LANES), F32), pltpu.VMEM((tm, LANES), F32)],
        compiler_params=_params("arbitrary", "arbitrary"),
        name=name,
    )(a, w, x, jnp.broadcast_to(g.reshape(1, n), (LN_ROWS, n)), jnp.broadcast_to(b.reshape(1, n), (LN_ROWS, n)),
      *[job[0] for job in cast_jobs])
    return res[0], res[1], res[2:]


def _rotary_inv_freq():
    half = B_ROT_DIM // 2
    inv_freq = jnp.float32(ROPE_THETA) ** (-jnp.arange(half, dtype=F32) * 2.0 / B_ROT_DIM)
    zeros = jnp.zeros((B_HEAD_DIM - B_ROT_DIM,), F32)
    return jnp.concatenate([-inv_freq, inv_freq, zeros]).reshape(1, B_HEAD_DIM)


def kernel(x, positions, a_w_in, a_ln_g, a_ln_b, a_w_s, a_b_s, a_w_out, b_w_in, b_w_out, c_w_in, c_w_group, c_scale, c_w_out, f_w_up, f_conv_w, f_conv_b, f_w_down, ln_mix_g, ln_mix_b, ln_ffn_g, ln_ffn_b):
    batch, seq, d = x.shape
    m = batch * seq
    xf = x.reshape(m, d)
    xb = xf.astype(BF16)
    posb = jnp.broadcast_to(positions.reshape(m, 1).astype(F32), (m, B_HEAD_DIM))
    invf = _rotary_inv_freq()

    c_w_group_2d = c_w_group.reshape(c_w_group.shape[0], -1, c_w_group.shape[-1])
    a_in = [a_w_in[0].astype(BF16), None]
    a_out = [None, None]
    ffn_up_w = None
    for layer in range(DEPTH):
        kind, idx = layer % N_MIXERS, layer // N_MIXERS
        if kind == 0:
            jobs = [(f_w_up, layer, True)] + ([(a_w_out, 0, False)] if layer == 0 else [])
            h, casts = _mm(xb, a_in[idx], epilogue=_gelu, out_dtype=BF16, name="gmlp_in_gelu", cast_jobs=jobs)
            ffn_up_w = casts[:2]
            if layer == 0:
                a_out[0] = casts[2]
            bs_exp = jnp.repeat(a_b_s[idx].T, A_GROUP_DIM, axis=1)
            y = _spatial_gate(h, a_ln_g[idx].reshape(1, d), a_ln_b[idx].reshape(1, d), a_w_s[idx], bs_exp)
            w_out = a_out[idx]
        elif kind == 1:
            outs, lses, dilations = [], [], []
            for group, (window, dilation) in enumerate(B_PATTERNS):
                assert window // dilation == B_QBLOCK
                qkv, _ = _qkv_proj(xb, b_in, group, posb, invf, dilation)
                jobs = [(f_w_up, layer, True)] if group == 0 else []
                o, lse, casts = _dilated_attention(qkv, seq, dilation, cast_jobs=jobs)
                if group == 0:
                    ffn_up_w = casts
                outs.append(o)
                lses.append(lse)
                dilations.append(dilation)
            y = _combine_groups(outs, lses, tuple(dilations), m)
            w_out = b_out
        else:
            h, ffn_up_w = _mm(xb, c_in, epilogue=lambda t: t, out_dtype=F32, name="pool_in",
                              cast_jobs=[(f_w_up, layer, True)])
            y = _pool_mix(h, c_group.reshape(c_w_group[idx].shape), c_scale[idx].reshape(1, d), seq)
            w_out = c_out
        xf, xb, _ = _proj_ln(y, w_out, xf, ln_mix_g[layer], ln_mix_b[layer], name="mixer_out_ln")

        jobs = [(f_w_down, layer, False)]
        nxt = layer + 1
        nxt_kind, nxt_idx = nxt % N_MIXERS, nxt // N_MIXERS
        if nxt < DEPTH and nxt_kind == 0:
            jobs += [(a_w_in, nxt_idx, False), (a_w_out, nxt_idx, False)]
        elif nxt < DEPTH and nxt_kind == 1:
            jobs += [(b_w_in, nxt_idx, False), (b_w_out, nxt_idx, False)]
        elif nxt < DEPTH:
            jobs += [(c_w_in, nxt_idx, False), (c_w_out, nxt_idx, False), (c_w_group_2d, nxt_idx, False)]
        cw, cb = f_conv_w[layer], f_conv_b[layer].reshape(1, -1)
        g, casts = _ffn_up(xb, ffn_up_w[0], ffn_up_w[1], cw[:, :D_FF], cw[:, D_FF:], cb[:, :D_FF], cb[:, D_FF:],
                           seq, cast_jobs=jobs)
        w_down = casts[0]
        if nxt < DEPTH and nxt_kind == 0:
            a_in[nxt_idx], a_out[nxt_idx] = casts[1:3]
        elif nxt < DEPTH and nxt_kind == 1:
            b_in, b_out = casts[1:3]
        elif nxt < DEPTH:
            c_in, c_out, c_group = casts[1:4]
        xf, xb, _ = _proj_ln(g, w_down, xf, ln_ffn_g[layer], ln_ffn_b[layer], name="ffn_down_ln")

    return xf.reshape(batch, seq, d)
```
